```python
import math
import jax, jax.numpy as jnp
from jax import lax
import numpy as np

D_MODEL = 4096
BATCH = 1
SEQ = 8192
DEPTH = 4

GRID_W = 64
CTX_LEN = 256
BLOCK = 128
N_MIXERS = 4
HEAD_DIM = 128
ROPE_BASE = 10000.0
EPS = 1e-6
NEG_INF = -1e30
N_MOD = 6

A_HEADS = D_MODEL // HEAD_DIM
A_KV_HEADS = A_HEADS // 4
WINDOW = 128
B_HEADS = D_MODEL // HEAD_DIM
B_KV_HEADS = B_HEADS // 4
C_HEADS = D_MODEL // HEAD_DIM
C_Q_RANK = D_MODEL // 4
C_KV_RANK = 512
C_NOPE = 128
C_ROPE = 64
C_V = 128
DF_HEAD = 64
DF_HEADS = D_MODEL // (2 * DF_HEAD)
N_GROUPS = 4
EXPERTS_PER_GROUP = 6
N_EXPERTS = N_GROUPS * EXPERTS_PER_GROUP
TOP_K = 2
D_EXPERT = D_MODEL // 16
ALPHA = (2.0 * DEPTH) ** 0.25
BETA = (8.0 * DEPTH) ** -0.25

kernel_name = "hybrid_interleaved_diffusion_block"


def rms_norm(x, g):
    xf = x.astype(jnp.float32)
    y = xf * lax.rsqrt(jnp.mean(xf * xf, axis=-1, keepdims=True) + EPS)
    return (y * g.astype(jnp.float32)).astype(x.dtype)


def layer_norm(x, g, b):
    xf = x.astype(jnp.float32)
    mu = jnp.mean(xf, axis=-1, keepdims=True)
    xc = xf - mu
    var = jnp.mean(xc * xc, axis=-1, keepdims=True)
    return (xc * lax.rsqrt(var + EPS) * g.astype(jnp.float32) + b.astype(jnp.float32)).astype(x.dtype)


def axial_rope_tables(n_tokens, rot_dim):
    rows = n_tokens // GRID_W
    r, col = jnp.meshgrid(jnp.arange(rows, dtype=jnp.float32), jnp.arange(GRID_W, dtype=jnp.float32), indexing="ij")
    pos = jnp.stack([r.reshape(-1), col.reshape(-1)], axis=-1)
    n_freq = rot_dim // 4
    inv_freq = ROPE_BASE ** (-jnp.arange(n_freq, dtype=jnp.float32) / n_freq)
    ang = pos[:, :, None] * inv_freq
    return jnp.cos(ang), jnp.sin(ang)


def apply_axial_rope(x, cos, sin):
    shape = x.shape
    nf = shape[-1] // 4
    xr = x.astype(jnp.float32).reshape(shape[:-1] + (2, 2, nf))
    bshape = (shape[1],) + (1,) * (x.ndim - 3) + (2, nf)
    cs, sn = cos.reshape(bshape), sin.reshape(bshape)
    x1, x2 = xr[..., 0, :], xr[..., 1, :]
    out = jnp.stack([x1 * cs - x2 * sn, x2 * cs + x1 * sn], axis=-2)
    return out.reshape(shape).astype(x.dtype)


def attn_probs(q, k, mask, sink):
    s = jnp.einsum("bqhgd,bkhd->bhgqk", q, k).astype(jnp.float32)
    if mask is not None:
        s = jnp.where(mask, s, NEG_INF)
    if sink is None:
        return jax.nn.softmax(s, axis=-1)
    sk = jnp.broadcast_to(sink.astype(jnp.float32)[None, :, :, None, None], s.shape[:-1] + (1,))
    return jax.nn.softmax(jnp.concatenate([s, sk], axis=-1), axis=-1)[..., :-1]


def attn_out(p, v):
    return jnp.einsum("bhgqk,bkhd->bqhgd", p.astype(v.dtype), v)


def sweep_query_blocks(fn, q):
    b, s = q.shape[:2]
    nb = s // BLOCK
    qb = jnp.moveaxis(q.reshape((b, nb, BLOCK) + q.shape[2:]), 1, 0)
    out = lax.map(lambda args: fn(args[0], args[1]), (qb, jnp.arange(nb)))
    out = jnp.moveaxis(out, 0, 1)
    return out.reshape((b, s) + out.shape[3:])


def split_gqa(t, b, n, n_heads, n_kv):
    q, k, v = jnp.split(t, [n_heads * HEAD_DIM, (n_heads + n_kv) * HEAD_DIM], axis=-1)
    return (q.reshape(b, n, n_kv, n_heads // n_kv, HEAD_DIM),
            k.reshape(b, n, n_kv, HEAD_DIM), v.reshape(b, n, n_kv, HEAD_DIM))


def window_sink_attention(hl, hc, w_qkv, w_o, sink, cos, sin, need_ctx):
    b, s, _ = hl.shape
    n_ctx = hc.shape[1]
    scale = HEAD_DIM ** -0.5
    ql, kl, vl = split_gqa(hl @ w_qkv, b, s, A_HEADS, A_KV_HEADS)
    qc, kc, vc = split_gqa(hc @ w_qkv, b, n_ctx, A_HEADS, A_KV_HEADS)
    ql = apply_axial_rope(ql, cos, sin) * scale
    kl = apply_axial_rope(kl, cos, sin)
    sink_hg = sink.reshape(A_KV_HEADS, A_HEADS // A_KV_HEADS)
    pad = ((0, 0), (BLOCK, BLOCK), (0, 0), (0, 0))
    kp, vp = jnp.pad(kl, pad), jnp.pad(vl, pad)
    rel = (jnp.arange(3 * BLOCK) - BLOCK)[None, :] - jnp.arange(BLOCK)[:, None]
    band = jnp.abs(rel) <= WINDOW
    ctx_mask = jnp.ones((BLOCK, n_ctx), dtype=bool)

    def block(qb, n):
        start = n * BLOCK
        kpos = start - BLOCK + jnp.arange(3 * BLOCK)
        inside = (kpos >= 0) & (kpos < s)
        mask = jnp.concatenate([band & inside[None, :], ctx_mask], axis=-1)
        kw = jnp.concatenate([lax.dynamic_slice_in_dim(kp, start, 3 * BLOCK, axis=1), kc], axis=1)
        vw = jnp.concatenate([lax.dynamic_slice_in_dim(vp, start, 3 * BLOCK, axis=1), vc], axis=1)
        return attn_out(attn_probs(qb, kw, mask, sink_hg), vw)

    out_l = sweep_query_blocks(block, ql).reshape(b, s, -1) @ w_o
    out_c = None
    if need_ctx:
        oc = attn_out(attn_probs(qc * scale, kc, None, sink_hg), vc)
        out_c = oc.reshape(b, n_ctx, -1) @ w_o
    return out_l, out_c


def qknorm_axial_attention(hl, hc, w_qkv, q_gain, k_gain, w_o, cos, sin, need_ctx):
    b, s, _ = hl.shape
    n_ctx = hc.shape[1]
    scale = HEAD_DIM ** -0.5
    ql, kl, vl = split_gqa(hl @ w_qkv, b, s, B_HEADS, B_KV_HEADS)
    qc, kc, vc = split_gqa(hc @ w_qkv, b, n_ctx, B_HEADS, B_KV_HEADS)
    ql = apply_axial_rope(rms_norm(ql, q_gain), cos, sin) * scale
    kl = apply_axial_rope(rms_norm(kl, k_gain), cos, sin)
    qc = rms_norm(qc, q_gain) * scale
    kc = rms_norm(kc, k_gain)
    k_all = jnp.concatenate([kl, kc], axis=1)
    v_all = jnp.concatenate([vl, vc], axis=1)
    ol = sweep_query_blocks(lambda qb, n: attn_out(attn_probs(qb, k_all, None, None), v_all), ql)
    out_l = ol.reshape(b, s, -1) @ w_o
    out_c = None
    if need_ctx:
        oc = attn_out(attn_probs(qc, kc, None, None), vc)
        out_c = oc.reshape(b, n_ctx, -1) @ w_o
    return out_l, out_c


def mla_attention(hl, hc, w_a, q_gain, kv_gain, w_qb, w_kvb, w_o, cos, sin, need_ctx):
    scale = (C_NOPE + C_ROPE) ** -0.5

    def project(h, rotary):
        b, n, _ = h.shape
        cq, ckv, k_pe = jnp.split(h @ w_a, [C_Q_RANK, C_Q_RANK + C_KV_RANK], axis=-1)
        q = (rms_norm(cq, q_gain) @ w_qb).reshape(b, n, C_HEADS, C_NOPE + C_ROPE)
        kv = (rms_norm(ckv, kv_gain) @ w_kvb).reshape(b, n, C_HEADS, C_NOPE + C_V)
        q_nope, q_pe = jnp.split(q, [C_NOPE], axis=-1)
        k_nope, v = jnp.split(kv, [C_NOPE], axis=-1)
        if rotary:
            q_pe = apply_axial_rope(q_pe, cos, sin)
            k_pe = apply_axial_rope(k_pe, cos, sin)
        k = jnp.concatenate([k_nope, jnp.broadcast_to(k_pe[:, :, None, :], (b, n, C_HEADS, C_ROPE))], axis=-1)
        q = jnp.concatenate([q_nope, q_pe], axis=-1)[:, :, :, None, :] * scale
        return q, k, v

    b, s, _ = hl.shape
    ql, kl, vl = project(hl, True)
    qc, kc, vc = project(hc, False)
    k_all = jnp.concatenate([kl, kc], axis=1)
    v_all = jnp.concatenate([vl, vc], axis=1)
    ol = sweep_query_blocks(lambda qb, n: attn_out(attn_probs(qb, k_all, None, None), v_all), ql)
    out_l = ol.reshape(b, s, -1) @ w_o
    out_c = None
    if need_ctx:
        oc = attn_out(attn_probs(qc, kc, None, None), vc)
        out_c = oc.reshape(b, hc.shape[1], -1) @ w_o
    return out_l, out_c


def diff_attend(q, k, v, lam, subln, lambda_init):
    s = jnp.einsum("bqhcd,bkhcd->bhcqk", q, k).astype(jnp.float32)
    p = jax.nn.softmax(s, axis=-1)
    a = p[:, :, 0] - lam * p[:, :, 1]
    o = jnp.einsum("bhqk,bkhd->bqhd", a.astype(v.dtype), v)
    return rms_norm(o, subln) * (1.0 - lambda_init)


def differential_attention(hl, hc, w_qkv, lam_vecs, subln, w_o, cos, sin, lambda_init, need_ctx):
    width = DF_HEADS * 2 * DF_HEAD

    def project(h, rotary):
        b, n, _ = h.shape
        q, k, v = jnp.split(h @ w_qkv, [width, 2 * width], axis=-1)
        q = q.reshape(b, n, DF_HEADS, 2, DF_HEAD)
        k = k.reshape(b, n, DF_HEADS, 2, DF_HEAD)
        if rotary:
            q = apply_axial_rope(q, cos, sin)
            k = apply_axial_rope(k, cos, sin)
        return q * DF_HEAD ** -0.5, k, v.reshape(b, n, DF_HEADS, 2 * DF_HEAD)

    lv = lam_vecs.astype(jnp.float32)
    lam = jnp.exp(jnp.sum(lv[0] * lv[1])) - jnp.exp(jnp.sum(lv[2] * lv[3])) + lambda_init
    b, s, _ = hl.shape
    ql, kl, vl = project(hl, True)
    qc, kc, vc = project(hc, False)
    k_all = jnp.concatenate([kl, kc], axis=1)
    v_all = jnp.concatenate([vl, vc], axis=1)
    ol = sweep_query_blocks(lambda qb, n: diff_attend(qb, k_all, v_all, lam, subln, lambda_init), ql)
    out_l = ol.reshape(b, s, -1) @ w_o
    out_c = None
    if need_ctx:
        out_c = diff_attend(qc, kc, vc, lam, subln, lambda_init).reshape(b, hc.shape[1], -1) @ w_o
    return out_l, out_c


def hierarchical_moe(h, w_group, b_group, w_expert, b_expert, w_gate_up, w_down):
    t = h.shape[0]
    rows = jnp.arange(t)
    g_logits = (h @ w_group).astype(jnp.float32) + b_group.astype(jnp.float32)
    g_prob = jax.nn.softmax(g_logits, axis=-1)
    g_idx = jnp.argmax(g_logits, axis=-1)
    g_w = g_prob[rows, g_idx]
    e_logits = ((h @ w_expert).astype(jnp.float32) + b_expert.astype(jnp.float32)).reshape(t, N_GROUPS, EXPERTS_PER_GROUP)
    e_in = e_logits[rows, g_idx]
    top_v, top_i = lax.top_k(e_in, TOP_K)
    top_w = jax.nn.softmax(top_v, axis=-1) * g_w[:, None]
    glob = g_idx[:, None] * EXPERTS_PER_GROUP + top_i
    combine = jnp.sum(jax.nn.one_hot(glob, N_EXPERTS, dtype=jnp.float32) * top_w[..., None], axis=1)
    gate, up = jnp.split(jnp.einsum("td,edf->tef", h, w_gate_up), [D_EXPERT], axis=-1)
    act = jax.nn.silu(gate) * up * combine[..., None].astype(h.dtype)
    return jnp.einsum("tef,efd->td", act, w_down)


def modulation(cond, w, b):
    m = jax.nn.silu(cond) @ w + b
    return m.reshape(cond.shape[:-1] + (N_MOD, D_MODEL))


def setup_inputs(seed: int = 0) -> dict:
    key = jax.random.key(seed)
    ks = iter(jax.random.split(key, 40))
    f32 = jnp.float32
    d = D_MODEL

    def nrm(shape, scale):
        return jax.random.normal(next(ks), shape, f32) * scale

    def gain(shape):
        return 1.0 + nrm(shape, 0.02)

    n_win, n_qkn, n_mla, n_diff = (len(range(k, DEPTH, N_MIXERS)) for k in range(N_MIXERS))
    return {
        "x": nrm((BATCH, SEQ, d), 1.0),
        "c": nrm((BATCH, d), 1.0),
        "ctx": nrm((BATCH, CTX_LEN, d), 1.0),
        "c_ctx": nrm((d,), 1.0),
        "ada_w": nrm((DEPTH, d, N_MOD * d), 0.5 * d ** -0.5),
        "ada_b": nrm((DEPTH, N_MOD * d), 0.02),
        "ln_g": gain((DEPTH, 2, d)),
        "ln_b": nrm((DEPTH, 2, d), 0.02),
        "win_w_qkv": nrm((n_win, d, (A_HEADS + 2 * A_KV_HEADS) * HEAD_DIM), d ** -0.5),
        "win_w_o": nrm((n_win, A_HEADS * HEAD_DIM, d), BETA * (A_HEADS * HEAD_DIM) ** -0.5),
        "win_sink": nrm((n_win, A_HEADS), 0.5),
        "qkn_w_qkv": nrm((n_qkn, d, (B_HEADS + 2 * B_KV_HEADS) * HEAD_DIM), d ** -0.5),
        "qkn_q_gain": gain((n_qkn, HEAD_DIM)),
        "qkn_k_gain": gain((n_qkn, HEAD_DIM)),
        "qkn_w_o": nrm((n_qkn, B_HEADS * HEAD_DIM, d), BETA * (B_HEADS * HEAD_DIM) ** -0.5),
        "mla_w_a": nrm((n_mla, d, C_Q_RANK + C_KV_RANK + C_ROPE), d ** -0.5),
        "mla_q_gain": gain((n_mla, C_Q_RANK)),
        "mla_kv_gain": gain((n_mla, C_KV_RANK)),
        "mla_w_qb": nrm((n_mla, C_Q_RANK, C_HEADS * (C_NOPE + C_ROPE)), C_Q_RANK ** -0.5),
        "mla_w_kvb": nrm((n_mla, C_KV_RANK, C_HEADS * (C_NOPE + C_V)), C_KV_RANK ** -0.5),
        "mla_w_o": nrm((n_mla, C_HEADS * C_V, d), BETA * (C_HEADS * C_V) ** -0.5),
        "diff_w_qkv": nrm((n_diff, d, 3 * DF_HEADS * 2 * DF_HEAD), d ** -0.5),
        "diff_lambda": nrm((n_diff, 4, DF_HEAD), 0.1),
        "diff_subln": gain((n_diff, 2 * DF_HEAD)),
        "diff_w_o": nrm((n_diff, DF_HEADS * 2 * DF_HEAD, d), BETA * (DF_HEADS * 2 * DF_HEAD) ** -0.5),
        "moe_w_group": nrm((DEPTH, d, N_GROUPS), d ** -0.5),
        "moe_b_group": nrm((DEPTH, N_GROUPS), 0.01),
        "moe_w_expert": nrm((DEPTH, d, N_EXPERTS), d ** -0.5),
        "moe_b_expert": nrm((DEPTH, N_EXPERTS), 0.01),
        "moe_w_gate_up": nrm((DEPTH, N_EXPERTS, d, 2 * D_EXPERT), d ** -0.5),
        "moe_w_down": nrm((DEPTH, N_EXPERTS, D_EXPERT, d), BETA * D_EXPERT ** -0.5),
    }


def reference(x, c, ctx, c_ctx, ada_w, ada_b, ln_g, ln_b,
              win_w_qkv, win_w_o, win_sink,
              qkn_w_qkv, qkn_q_gain, qkn_k_gain, qkn_w_o,
              mla_w_a, mla_q_gain, mla_kv_gain, mla_w_qb, mla_w_kvb, mla_w_o,
              diff_w_qkv, diff_lambda, diff_subln, diff_w_o,
              moe_w_group, moe_b_group, moe_w_expert, moe_b_expert, moe_w_gate_up, moe_w_down):
    b, s, d = x.shape
    n_ctx = ctx.shape[1]
    cos_h, sin_h = axial_rope_tables(s, HEAD_DIM)
    cos_r, sin_r = axial_rope_tables(s, C_ROPE)
    xl, xc = x, ctx
    for i in range(DEPTH):
        kind, j = i % N_MIXERS, i // N_MIXERS
        need_ctx = i < DEPTH - 1
        ml = modulation(c, ada_w[i], ada_b[i])[:, :, None, :]
        mc = modulation(c_ctx, ada_w[i], ada_b[i])
        hl = xl * (1.0 + ml[:, 1]) + ml[:, 0]
        hc = xc * (1.0 + mc[1]) + mc[0]
        if kind == 0:
            al, ac = window_sink_attention(hl, hc, win_w_qkv[j], win_w_o[j], win_sink[j], cos_h, sin_h, need_ctx)
        elif kind == 1:
            al, ac = qknorm_axial_attention(hl, hc, qkn_w_qkv[j], qkn_q_gain[j], qkn_k_gain[j], qkn_w_o[j],
                                            cos_h, sin_h, need_ctx)
        elif kind == 2:
            al, ac = mla_attention(hl, hc, mla_w_a[j], mla_q_gain[j], mla_kv_gain[j], mla_w_qb[j], mla_w_kvb[j],
                                   mla_w_o[j], cos_r, sin_r, need_ctx)
        else:
            lambda_init = 0.8 - 0.6 * math.exp(-0.3 * i)
            al, ac = differential_attention(hl, hc, diff_w_qkv[j], diff_lambda[j], diff_subln[j], diff_w_o[j],
                                            cos_r, sin_r, lambda_init, need_ctx)
        xl = layer_norm(ALPHA * xl + ml[:, 2] * al, ln_g[i, 0], ln_b[i, 0])
        hl = xl * (1.0 + ml[:, 4]) + ml[:, 3]
        moe_args = (moe_w_group[i], moe_b_group[i], moe_w_expert[i], moe_b_expert[i], moe_w_gate_up[i], moe_w_down[i])
        if need_ctx:
            xc = layer_norm(ALPHA * xc + mc[2] * ac, ln_g[i, 0], ln_b[i, 0])
            hc = xc * (1.0 + mc[4]) + mc[3]
            tokens = jnp.concatenate([hc, hl], axis=1).reshape(-1, d)
            y = hierarchical_moe(tokens, *moe_args).reshape(b, n_ctx + s, d)
            yc, yl = y[:, :n_ctx], y[:, n_ctx:]
            xc = layer_norm(ALPHA * xc + mc[5] * yc, ln_g[i, 1], ln_b[i, 1])
        else:
            yl = hierarchical_moe(hl.reshape(-1, d), *moe_args).reshape(b, s, d)
        xl = layer_norm(ALPHA * xl + ml[:, 5] * yl, ln_g[i, 1], ln_b[i, 1])
    return xl
```

```python
import functools
import math

import jax
import jax.numpy as jnp
from jax import lax
from jax.experimental import pallas as pl
from jax.experimental.pallas import tpu as pltpu

F32 = jnp.float32
BF16 = jnp.bfloat16

DEPTH = 4
GRID_W = 64
HEAD_DIM = 128
ROPE_BASE = 10000.0
EPS = 1e-6
NEG_INF = -1e30
N_MOD = 6
WINDOW = 128
C_NOPE = 128
C_ROPE = 64
C_V = 128
DF_HEAD = 64
N_GROUPS = 4
EXPERTS_PER_GROUP = 6
N_EXPERTS = N_GROUPS * EXPERTS_PER_GROUP
ALPHA = (2.0 * DEPTH) ** 0.25
LANES = 128
VMEM_LIMIT_BYTES = 56 * 1024 * 1024


def _pick(n, cands):
    for c in cands:
        if n % c == 0:
            return c
    raise ValueError(f"no tile in {cands} divides {n}")


def _params(*sem):
    return pltpu.CompilerParams(dimension_semantics=sem, vmem_limit_bytes=VMEM_LIMIT_BYTES)


def _mod_kernel(cond_ref, w_ref, b_ref, o_ref):
    a = cond_ref[...]
    a = (a * jax.nn.sigmoid(a)).astype(BF16)
    o_ref[0] = jnp.dot(a, w_ref[0].astype(BF16), preferred_element_type=F32) + b_ref[0]


def modulation_all(cond8, ada_w, ada_b):
    depth, d, n = ada_w.shape
    tn = _pick(n, (512, 256, 128))
    return pl.pallas_call(
        _mod_kernel,
        grid=(depth, n // tn),
        in_specs=[pl.BlockSpec((8, d), lambda l, j: (0, 0)),
                  pl.BlockSpec((1, d, tn), lambda l, j: (l, 0, j)),
                  pl.BlockSpec((1, 1, tn), lambda l, j: (l, 0, j))],
        out_specs=pl.BlockSpec((1, 8, tn), lambda l, j: (l, 0, j)),
        out_shape=jax.ShapeDtypeStruct((depth, 8, n), F32),
        compiler_params=_params("parallel", "parallel"),
        name="modulation",
    )(cond8, ada_w, ada_b.reshape(depth, 1, n))


def _modulate_kernel(x_ref, v_ref, h_ref):
    v = v_ref[0]
    h_ref[...] = (x_ref[...] * (1.0 + v[4:5]) + v[3:4]).astype(h_ref.dtype)


def _ln_mod_kernel(x_ref, y_ref, v_ref, xo_ref, *h_refs):
    v = v_ref[0]
    z = ALPHA * x_ref[...] + v[0:1] * y_ref[...].astype(F32)
    mu = jnp.mean(z, axis=-1, keepdims=True)
    zc = z - mu
    var = jnp.mean(zc * zc, axis=-1, keepdims=True)
    xn = zc * lax.rsqrt(var + EPS) * v[1:2] + v[2:3]
    xo_ref[...] = xn
    if h_refs:
        h_refs[0][...] = (xn * (1.0 + v[4:5]) + v[3:4]).astype(h_refs[0].dtype)


def _row_specs(tr, d, n_lat_tiles):
    row = pl.BlockSpec((tr, d), lambda i: (i, 0))
    vec = pl.BlockSpec((1, 8, d), lambda i: ((i >= n_lat_tiles).astype(jnp.int32), 0, 0))
    return row, vec


def modulate(x, vecs, n_lat):
    t, d = x.shape
    tr = _pick(math.gcd(t, n_lat), (256, 128, 64, 32, 16, 8))
    row, vec = _row_specs(tr, d, n_lat // tr)
    return pl.pallas_call(
        _modulate_kernel, grid=(t // tr,), in_specs=[row, vec], out_specs=row,
        out_shape=jax.ShapeDtypeStruct((t, d), BF16),
        compiler_params=_params("parallel"), name="modulate",
    )(x, vecs)


def ln_mod(x, y, vecs, n_lat, rows, emit_h=True):
    d = x.shape[1]
    tr = _pick(math.gcd(rows, n_lat), (256, 128, 64, 32, 16, 8))
    row, vec = _row_specs(tr, d, n_lat // tr)
    out_shape = [jax.ShapeDtypeStruct((rows, d), F32)]
    out_specs = [row]
    if emit_h:
        out_shape.append(jax.ShapeDtypeStruct((rows, d), BF16))
        out_specs.append(row)
    return pl.pallas_call(
        _ln_mod_kernel, grid=(rows // tr,), in_specs=[row, row, vec], out_specs=out_specs,
        out_shape=out_shape, compiler_params=_params("parallel"), name="ln_mod",
    )(x, y, vecs)


def _mm_kernel(*refs, n_extra, prologue, epilogue):
    a_ref, w_ref = refs[0], refs[1]
    extras = refs[2:2 + n_extra]
    o_ref, wb_ref = refs[2 + n_extra], refs[3 + n_extra]

    @pl.when(pl.program_id(1) == 0)
    def _():
        wb_ref[...] = w_ref[...].astype(BF16)

    a = a_ref[...]
    if prologue is not None:
        a = prologue(a, *extras)
    acc = jnp.dot(a.astype(BF16), wb_ref[...], preferred_element_type=F32)
    if epilogue is not None:
        acc = epilogue(acc, *extras)
    o_ref[...] = acc.astype(o_ref.dtype)


def matmul(a, w, *, rows, col0, n_cols, out_dtype, tn=None, prologue=None, epilogue=None,
           extras=(), extra_specs=(), name="matmul"):
    k = a.shape[1]
    tm = _pick(rows, (1024, 768, 512, 256, 128))
    if tn is None:
        tn = _pick(math.gcd(n_cols, col0) if col0 else n_cols, (512, 256, 128))
    cb0 = col0 // tn
    kern = functools.partial(_mm_kernel, n_extra=len(extras), prologue=prologue, epilogue=epilogue)
    return pl.pallas_call(
        kern,
        grid=(n_cols // tn, rows // tm),
        in_specs=[pl.BlockSpec((tm, k), lambda j, i: (i, 0)),
                  pl.BlockSpec((k, tn), lambda j, i: (0, cb0 + j)),
                  *extra_specs],
        out_specs=pl.BlockSpec((tm, tn), lambda j, i: (i, j)),
        out_shape=jax.ShapeDtypeStruct((rows, n_cols), out_dtype),
        scratch_shapes=[pltpu.VMEM((k, tn), BF16)],
        compiler_params=_params("parallel", "arbitrary"),
        name=name,
    )(a, w, *extras)


def _swap_pairs(x, half):
    n = x.shape[-1]
    lane = lax.broadcasted_iota(jnp.int32, x.shape, x.ndim - 1)
    fwd = pltpu.roll(x, n - half, axis=x.ndim - 1)
    bwd = pltpu.roll(x, half, axis=x.ndim - 1)
    return jnp.where((lane % (2 * half)) < half, fwd, bwd)


def _rope(x, cos, sin_signed, half):
    return x * cos + _swap_pairs(x, half) * sin_signed


def _rms(x, gain):
    return x * lax.rsqrt(jnp.mean(x * x, axis=-1, keepdims=True) + EPS) * gain


def _head_epilogue(acc, cos_ref, sin_ref, gain_ref, *, half, scale, norm, rope_chunks):
    outs = []
    for c in range(acc.shape[1] // LANES):
        x = acc[:, c * LANES:(c + 1) * LANES]
        if norm:
            x = _rms(x, gain_ref[...])
        if rope_chunks is None or (c % rope_chunks[1]) == rope_chunks[0]:
            x = _rope(x, cos_ref[...], sin_ref[...], half)
        if scale != 1.0:
            x = x * scale
        outs.append(x)
    return jnp.concatenate(outs, axis=1) if len(outs) > 1 else outs[0]


def project_heads(a, w, tables, gain, *, rows, col0, n_cols, half, scale, norm=False,
                  rope_chunks=None, name="proj"):
    cos, sin = tables
    tm = _pick(rows, (1024, 768, 512, 256, 128))
    if gain is None:
        gain = jnp.ones((1, LANES), F32)
    epi = functools.partial(_head_epilogue, half=half, scale=scale, norm=norm, rope_chunks=rope_chunks)
    return matmul(
        a, w, rows=rows, col0=col0, n_cols=n_cols, out_dtype=BF16, epilogue=epi,
        extras=(cos, sin, gain),
        extra_specs=(pl.BlockSpec((tm, LANES), lambda j, i: (i, 0)),
                     pl.BlockSpec((tm, LANES), lambda j, i: (i, 0)),
                     pl.BlockSpec((1, LANES), lambda j, i: (0, 0))),
        name=name)


def _rms_prologue(a, gain_ref):
    return _rms(a, gain_ref[...])


def rms_matmul(a, gain, w, *, rows, n_cols, out_dtype, epilogue=None, extras=(), extra_specs=(), name="rms_mm"):
    k = a.shape[1]
    pro = lambda a_t, g_ref, *rest: _rms_prologue(a_t, g_ref)
    epi = None if epilogue is None else (lambda acc, g_ref, *rest: epilogue(acc, *rest))
    return matmul(a, w, rows=rows, col0=0, n_cols=n_cols, out_dtype=out_dtype, prologue=pro, epilogue=epi,
                  extras=(gain.reshape(1, k), *extras),
                  extra_specs=(pl.BlockSpec((1, k), lambda j, i: (0, 0)), *extra_specs), name=name)


def _softmax_tile(s, v, g, m_scr, l_scr, acc_scr):
    m_prev = m_scr[g]
    m_new = jnp.maximum(m_prev, jnp.max(s, axis=1, keepdims=True))
    p = jnp.exp(s - m_new[:, :1])
    alpha = jnp.exp(m_prev - m_new)
    l_scr[g] = alpha * l_scr[g] + jnp.sum(p, axis=1, keepdims=True)
    acc_scr[g] = acc_scr[g] * alpha[:, :1] + jnp.dot(p.astype(BF16), v, preferred_element_type=F32)
    m_scr[g] = m_new


def _qk(q, k):
    return lax.dot_general(q, k, (((1,), (1,)), ((), ())), preferred_element_type=F32)


def _flash_kernel(*refs, G, dq, dv, has_k2, has_ctx, has_sink, nk):
    refs = list(refs)
    sink_ref = refs.pop(0) if has_sink else None
    q_ref = refs.pop(0)
    k_ref = refs.pop(0)
    k2_ref = refs.pop(0) if has_k2 else None
    v_ref = refs.pop(0)
    if has_ctx:
        kc_ref = refs.pop(0)
        kc2_ref = refs.pop(0) if has_k2 else None
        vc_ref = refs.pop(0)
    o_ref, m_scr, l_scr, acc_scr = refs[-4:]
    h, j = pl.program_id(0), pl.program_id(2)

    def tile(k_r, k2_r, v_r):
        k = k_r[...]
        if k2_r is not None:
            k = jnp.concatenate([k, k2_r[...]], axis=1)
        v = v_r[...]
        for g in range(G):
            _softmax_tile(_qk(q_ref[:, g * dq:(g + 1) * dq], k), v, g, m_scr, l_scr, acc_scr)

    @pl.when(j == 0)
    def _():
        for g in range(G):
            if has_sink:
                m_scr[g] = jnp.full(m_scr.shape[1:], sink_ref[h * G + g], F32)
                l_scr[g] = jnp.ones(l_scr.shape[1:], F32)
            else:
                m_scr[g] = jnp.full(m_scr.shape[1:], NEG_INF, F32)
                l_scr[g] = jnp.zeros(l_scr.shape[1:], F32)
        acc_scr[...] = jnp.zeros(acc_scr.shape, F32)
        if has_ctx:
            tile(kc_ref, kc2_ref, vc_ref)

    tile(k_ref, k2_ref, v_ref)

    @pl.when(j == nk - 1)
    def _():
        for g in range(G):
            o_ref[:, g * dv:(g + 1) * dv] = (acc_scr[g] / l_scr[g][:, :1]).astype(o_ref.dtype)


def flash_attention(q, k, v, *, k2=None, sink=None, n_heads_kv, G, dq, dv, k_col, v_col,
                    q_row0, q_rows, k_row0, k_rows, ctx_row0=None, ctx_rows=None,
                    out_rows, out=None, name="flash"):
    tq = _pick(math.gcd(q_rows, q_row0) if q_row0 else q_rows, (512, 256, 128))
    tk = _pick(math.gcd(k_rows, k_row0) if k_row0 else k_rows, (512, 256, 128))
    nq, nk = q_rows // tq, k_rows // tk
    qb0, kb0 = q_row0 // tq, k_row0 // tk
    dk = k.shape[1] if False else LANES
    has_ctx = ctx_row0 is not None
    has_k2 = k2 is not None
    has_sink = sink is not None
    args, specs = [], []
    if has_sink:
        args.append(sink)
        specs.append(pl.BlockSpec(memory_space=pltpu.SMEM))
    args.append(q)
    specs.append(pl.BlockSpec((tq, G * dq), lambda h, i, j: (qb0 + i, h)))
    args.append(k)
    specs.append(pl.BlockSpec((tk, dk), lambda h, i, j: (kb0 + j, k_col(h))))
    if has_k2:
        args.append(k2)
        specs.append(pl.BlockSpec((tk, LANES), lambda h, i, j: (kb0 + j, 0)))
    args.append(v)
    specs.append(pl.BlockSpec((tk, dv), lambda h, i, j: (kb0 + j, v_col(h))))
    if has_ctx:
        cb = ctx_row0 // ctx_rows
        args.append(k)
        specs.append(pl.BlockSpec((ctx_rows, dk), lambda h, i, j: (cb, k_col(h))))
        if has_k2:
            args.append(k2)
            specs.append(pl.BlockSpec((ctx_rows, LANES), lambda h, i, j: (cb, 0)))
        args.append(v)
        specs.append(pl.BlockSpec((ctx_rows, dv), lambda h, i, j: (cb, v_col(h))))
    aliases = {}
    if out is not None:
        aliases = {len(args): 0}
        args.append(out)
        specs.append(pl.BlockSpec(memory_space=pl.ANY))
    kern = functools.partial(_flash_kernel, G=G, dq=dq, dv=dv, has_k2=has_k2, has_ctx=has_ctx,
                             has_sink=has_sink, nk=nk)
    if out is not None:
        kern = _drop_alias_ref(kern, len(args) - 1)
    return pl.pallas_call(
        kern,
        grid=(n_heads_kv, nq, nk),
        in_specs=specs,
        out_specs=pl.BlockSpec((tq, G * dv), lambda h, i, j: (qb0 + i, h)),
        out_shape=jax.ShapeDtypeStruct((out_rows, n_heads_kv * G * dv), BF16),
        scratch_shapes=[pltpu.VMEM((G, tq, LANES), F32), pltpu.VMEM((G, tq, LANES), F32),
                        pltpu.VMEM((G, tq, dv), F32)],
        input_output_aliases=aliases,
        compiler_params=_params("parallel", "parallel", "arbitrary"),
        name=name,
    )(*args)


def _drop_alias_ref(kern, idx):
    def wrapped(*refs):
        refs = list(refs)
        del refs[idx]
        return kern(*refs)
    return wrapped


def _window_kernel(sink_ref, q_ref, kp_ref, kc_ref, kn_ref, kx_ref, vp_ref, vc_ref, vn_ref, vx_ref,
                   o_ref, *, G, tq, nq):
    h, i = pl.program_id(0), pl.program_id(1)
    k = jnp.concatenate([kp_ref[...], kc_ref[...], kn_ref[...], kx_ref[...]], axis=0)
    v = jnp.concatenate([vp_ref[...], vc_ref[...], vn_ref[...], vx_ref[...]], axis=0)
    n_band = tq + 2 * WINDOW
    r = lax.broadcasted_iota(jnp.int32, (tq, k.shape[0]), 0)
    c = lax.broadcasted_iota(jnp.int32, (tq, k.shape[0]), 1)
    rel = c - WINDOW - r
    kpos = i * tq - WINDOW + c
    band = (jnp.abs(rel) <= WINDOW) & (kpos >= 0) & (kpos < nq * tq)
    mask = band | (c >= n_band)
    for g in range(G):
        s = _qk(q_ref[:, g * HEAD_DIM:(g + 1) * HEAD_DIM], k)
        s = jnp.where(mask, s, NEG_INF)
        sk = sink_ref[h * G + g]
        m = jnp.maximum(jnp.max(s, axis=1, keepdims=True), sk)
        p = jnp.exp(s - m)
        l = jnp.sum(p, axis=1, keepdims=True) + jnp.exp(sk - m)
        o = jnp.dot(p.astype(BF16), v, preferred_element_type=F32) / l
        o_ref[:, g * HEAD_DIM:(g + 1) * HEAD_DIM] = o.astype(o_ref.dtype)


def window_attention(q, k, v, sink, *, n_heads_kv, G, n_lat, n_ctx, out_rows):
    tq = 2 * WINDOW
    nq = n_lat // tq
    nb = n_lat // WINDOW
    cb = n_lat // n_ctx
    w = WINDOW

    def prev_map(h, i):
        return (jnp.maximum(2 * i - 1, 0), h)

    def next_map(h, i):
        return (jnp.minimum(2 * i + 2, nb - 1), h)

    kv_specs = [pl.BlockSpec((w, HEAD_DIM), prev_map),
                pl.BlockSpec((tq, HEAD_DIM), lambda h, i: (i, h)),
                pl.BlockSpec((w, HEAD_DIM), next_map),
                pl.BlockSpec((n_ctx, HEAD_DIM), lambda h, i: (cb, h))]
    kern = functools.partial(_window_kernel, G=G, tq=tq, nq=nq)
    return pl.pallas_call(
        kern,
        grid=(n_heads_kv, nq),
        in_specs=[pl.BlockSpec(memory_space=pltpu.SMEM),
                  pl.BlockSpec((tq, G * HEAD_DIM), lambda h, i: (i, h)),
                  *kv_specs, *kv_specs],
        out_specs=pl.BlockSpec((tq, G * HEAD_DIM), lambda h, i: (i, h)),
        out_shape=jax.ShapeDtypeStruct((out_rows, n_heads_kv * G * HEAD_DIM), BF16),
        compiler_params=_params("parallel", "parallel"),
        name="window_attention",
    )(sink, q, k, k, k, k, v, v, v, v)


def _diff_kernel(*refs, has_ctx, nk, lambda_init):
    refs = list(refs)
    lam_ref, sub_ref, q_ref, k_ref, v_ref = refs[:5]
    if has_ctx:
        kc_ref, vc_ref = refs[5:7]
    o_ref, m_scr, l_scr, acc_scr = refs[-4:]
    j = pl.program_id(2)
    lane = lax.broadcasted_iota(jnp.int32, q_ref.shape, 1)

    def tile(k_r, v_r):
        q = q_ref[...]
        k = k_r[...]
        v = v_r[...]
        zero = jnp.zeros_like(q)
        _softmax_tile(_qk(jnp.where(lane < DF_HEAD, q, zero), k), v, 0, m_scr, l_scr, acc_scr)
        _softmax_tile(_qk(jnp.where(lane >= DF_HEAD, q, zero), k), v, 1, m_scr, l_scr, acc_scr)

    @pl.when(j == 0)
    def _():
        m_scr[...] = jnp.full(m_scr.shape, NEG_INF, F32)
        l_scr[...] = jnp.zeros(l_scr.shape, F32)
        acc_scr[...] = jnp.zeros(acc_scr.shape, F32)
        if has_ctx:
            tile(kc_ref, vc_ref)

    tile(k_ref, v_ref)

    @pl.when(j == nk - 1)
    def _():
        lv = lam_ref[...]
        lam = (jnp.exp(jnp.sum(lv[0:1] * lv[1:2], axis=1, keepdims=True))
               - jnp.exp(jnp.sum(lv[2:3] * lv[3:4], axis=1, keepdims=True)) + lambda_init)
        o = acc_scr[0] / l_scr[0][:, :1] - lam * (acc_scr[1] / l_scr[1][:, :1])
        o_ref[...] = (_rms(o, sub_ref[...]) * (1.0 - lambda_init)).astype(o_ref.dtype)


def diff_attention(q, k, v, lam_vecs, subln, *, n_heads, lambda_init, q_row0, q_rows, k_row0, k_rows,
                   ctx_row0=None, ctx_rows=None, out_rows, out=None, name="diff_attention"):
    d = 2 * DF_HEAD
    tq = _pick(math.gcd(q_rows, q_row0) if q_row0 else q_rows, (512, 256, 128))
    tk = _pick(math.gcd(k_rows, k_row0) if k_row0 else k_rows, (512, 256, 128))
    nq, nk = q_rows // tq, k_rows // tk
    qb0, kb0 = q_row0 // tq, k_row0 // tk
    has_ctx = ctx_row0 is not None
    args = [lam_vecs, subln.reshape(1, d), q, k, v]
    specs = [pl.BlockSpec(lam_vecs.shape, lambda h, i, j: (0, 0)),
             pl.BlockSpec((1, d), lambda h, i, j: (0, 0)),
             pl.BlockSpec((tq, d), lambda h, i, j: (qb0 + i, h)),
             pl.BlockSpec((tk, d), lambda h, i, j: (kb0 + j, h)),
             pl.BlockSpec((tk, d), lambda h, i, j: (kb0 + j, h))]
    if has_ctx:
        cb = ctx_row0 // ctx_rows
        args += [k, v]
        specs += [pl.BlockSpec((ctx_rows, d), lambda h, i, j: (cb, h)),
                  pl.BlockSpec((ctx_rows, d), lambda h, i, j: (cb, h))]
    aliases = {}
    kern = functools.partial(_diff_kernel, has_ctx=has_ctx, nk=nk, lambda_init=lambda_init)
    if out is not None:
        aliases = {len(args): 0}
        args.append(out)
        specs.append(pl.BlockSpec(memory_space=pl.ANY))
        kern = _drop_alias_ref(kern, len(args) - 1)
    return pl.pallas_call(
        kern,
        grid=(n_heads, nq, nk),
        in_specs=specs,
        out_specs=pl.BlockSpec((tq, d), lambda h, i, j: (qb0 + i, h)),
        out_shape=jax.ShapeDtypeStruct((out_rows, n_heads * d), BF16),
        scratch_shapes=[pltpu.VMEM((2, tq, LANES), F32), pltpu.VMEM((2, tq, LANES), F32),
                        pltpu.VMEM((2, tq, d), F32)],
        input_output_aliases=aliases,
        compiler_params=_params("parallel", "parallel", "arbitrary"),
        name=name,
    )(*args)


def _router_kernel(h_ref, w_ref, b_ref, comb_ref):
    logits = jnp.dot(h_ref[...], w_ref[...].astype(BF16), preferred_element_type=F32) + b_ref[...]
    lane = lax.broadcasted_iota(jnp.int32, logits.shape, 1)
    big = jnp.int32(1 << 20)
    is_group = (lane >= N_EXPERTS) & (lane < N_EXPERTS + N_GROUPS)
    gl = jnp.where(is_group, logits, NEG_INF)
    gmax = jnp.max(gl, axis=1, keepdims=True)
    g_idx = jnp.min(jnp.where(gl == gmax, lane, big), axis=1, keepdims=True) - N_EXPERTS
    g_w = 1.0 / jnp.sum(jnp.exp(gl - gmax), axis=1, keepdims=True)
    lo = g_idx * EXPERTS_PER_GROUP
    el = jnp.where((lane >= lo) & (lane < lo + EXPERTS_PER_GROUP), logits, NEG_INF)
    v1 = jnp.max(el, axis=1, keepdims=True)
    i1 = jnp.min(jnp.where(el == v1, lane, big), axis=1, keepdims=True)
    el2 = jnp.where(lane == i1, NEG_INF, el)
    v2 = jnp.max(el2, axis=1, keepdims=True)
    i2 = jnp.min(jnp.where(el2 == v2, lane, big), axis=1, keepdims=True)
    e2 = jnp.exp(v2 - v1)
    den = 1.0 + e2
    w1 = (1.0 / den) * g_w
    w2 = (e2 / den) * g_w
    comb_ref[...] = jnp.where(lane == i1, w1, 0.0) + jnp.where(lane == i2, w2, 0.0)


def moe_router(h, w_r, b_r, rows):
    d = h.shape[1]
    tm = _pick(rows, (768, 512, 256, 128))
    return pl.pallas_call(
        _router_kernel, grid=(rows // tm,),
        in_specs=[pl.BlockSpec((tm, d), lambda i: (i, 0)),
                  pl.BlockSpec((d, LANES), lambda i: (0, 0)),
                  pl.BlockSpec((1, LANES), lambda i: (0, 0))],
        out_specs=pl.BlockSpec((tm, LANES), lambda i: (i, 0)),
        out_shape=jax.ShapeDtypeStruct((rows, LANES), F32),
        compiler_params=_params("parallel"), name="moe_router",
    )(h, w_r, b_r)


def _moe_dense_kernel(h_ref, comb_ref, wgu_ref, wd_ref, o_ref):
    e = pl.program_id(1)

    @pl.when(e == 0)
    def _():
        o_ref[...] = jnp.zeros(o_ref.shape, o_ref.dtype)

    gu = jnp.dot(h_ref[...], wgu_ref[0, 0].astype(BF16), preferred_element_type=F32)
    f = gu.shape[1] // 2
    gate, up = gu[:, :f], gu[:, f:]
    comb = comb_ref[...]
    lane = lax.broadcasted_iota(jnp.int32, comb.shape, 1)
    c = jnp.sum(jnp.where(lane == e, comb, 0.0), axis=1, keepdims=True)
    act = (gate * jax.nn.sigmoid(gate)) * up * c
    o_ref[...] += jnp.dot(act.astype(BF16), wd_ref[0, 0].astype(BF16), preferred_element_type=F32)


def moe_dense(h, comb, w_gate_up, w_down, layer, rows):
    d = h.shape[1]
    n_e, f2 = w_gate_up.shape[1], w_gate_up.shape[3]
    tm = _pick(rows, (384, 256, 128))
    return pl.pallas_call(
        _moe_dense_kernel, grid=(rows // tm, n_e),
        in_specs=[pl.BlockSpec((tm, d), lambda i, e: (i, 0)),
                  pl.BlockSpec((tm, LANES), lambda i, e: (i, 0)),
                  pl.BlockSpec((1, 1, d, f2), lambda i, e: (layer, e, 0, 0)),
                  pl.BlockSpec((1, 1, f2 // 2, d), lambda i, e: (layer, e, 0, 0))],
        out_specs=pl.BlockSpec((tm, d), lambda i, e: (i, 0)),
        out_shape=jax.ShapeDtypeStruct((rows, d), F32),
        compiler_params=_params("parallel", "arbitrary"), name="moe_dense",
    )(h, comb, w_gate_up, w_down)


def _rope_tables(n_lat, n_ctx, rot_dim):
    rows = n_lat // GRID_W
    r, col = jnp.meshgrid(jnp.arange(rows, dtype=F32), jnp.arange(GRID_W, dtype=F32), indexing="ij")
    pos = jnp.stack([r.reshape(-1), col.reshape(-1)], axis=-1)
    n_freq = rot_dim // 4
    inv_freq = ROPE_BASE ** (-jnp.arange(n_freq, dtype=F32) / n_freq)
    ang = pos[:, :, None] * inv_freq
    cos, sin = jnp.cos(ang), jnp.sin(ang)
    cos_full = jnp.concatenate([cos[:, 0], cos[:, 0], cos[:, 1], cos[:, 1]], axis=-1)
    sin_full = jnp.concatenate([-sin[:, 0], sin[:, 0], -sin[:, 1], sin[:, 1]], axis=-1)
    reps = LANES // rot_dim
    cos_full = jnp.tile(cos_full, (1, reps))
    sin_full = jnp.tile(sin_full, (1, reps))
    cos_full = jnp.concatenate([cos_full, jnp.ones((n_ctx, LANES), F32)], axis=0)
    sin_full = jnp.concatenate([sin_full, jnp.zeros((n_ctx, LANES), F32)], axis=0)
    return cos_full, sin_full


def _vec_pack(mods, gate_idx, ln_g, ln_b, next_mods, shift_idx, scale_idx):
    d = mods.shape[-1]
    z = jnp.zeros((2, d), F32)
    gate = mods[:2, gate_idx] if gate_idx is not None else z
    g = jnp.broadcast_to(ln_g, (2, d)) if ln_g is not None else z
    b = jnp.broadcast_to(ln_b, (2, d)) if ln_b is not None else z
    shift = next_mods[:2, shift_idx] if next_mods is not None else z
    scale = next_mods[:2, scale_idx] if next_mods is not None else z
    return jnp.stack([gate, g, b, shift, scale, z, z, z], axis=1)


def kernel(x, c, ctx, c_ctx, ada_w, ada_b, ln_g, ln_b, win_w_qkv, win_w_o, win_sink, qkn_w_qkv, qkn_q_gain, qkn_k_gain, qkn_w_o, mla_w_a, mla_q_gain, mla_kv_gain, mla_w_qb, mla_w_kvb, mla_w_o, diff_w_qkv, diff_lambda, diff_subln, diff_w_o, moe_w_group, moe_b_group, moe_w_expert, moe_b_expert, moe_w_gate_up, moe_w_down):
    b, n_lat, d = x.shape
    n_ctx = ctx.shape[1]
    assert b == 1
    t = n_lat + n_ctx
    n_heads = d // HEAD_DIM
    n_kv = n_heads // 4
    grp = n_heads // n_kv

    tab_h = _rope_tables(n_lat, n_ctx, HEAD_DIM)
    tab_r = _rope_tables(n_lat, n_ctx, C_ROPE)

    cond8 = jnp.zeros((8, d), F32).at[0].set(c[0]).at[1].set(c_ctx)
    mods = modulation_all(cond8, ada_w, ada_b).reshape(DEPTH, 8, N_MOD, d)

    xs = jnp.concatenate([x[0], ctx[0]], axis=0)
    h = modulate(xs, _vec_pack(mods[0], None, None, None, mods[0], 0, 1), n_lat)

    for i in range(DEPTH):
        kind = i % 4
        need_ctx = i < DEPTH - 1
        rows = t if need_ctx else n_lat
        ctx_kw = dict(q_row0=n_lat, q_rows=n_ctx, k_row0=n_lat, k_rows=n_ctx, out_rows=t)
        lat_kw = dict(q_row0=0, q_rows=n_lat, k_row0=0, k_rows=n_lat, ctx_row0=n_lat, ctx_rows=n_ctx,
                      out_rows=rows)
        if kind == 0:
            w = win_w_qkv[0]
            scale = HEAD_DIM ** -0.5
            q = project_heads(h, w, tab_h, None, rows=rows, col0=0, n_cols=d, half=32, scale=scale, name="win_q")
            k = project_heads(h, w, tab_h, None, rows=t, col0=d, n_cols=n_kv * HEAD_DIM, half=32, scale=1.0,
                              name="win_k")
            v = matmul(h, w, rows=t, col0=d + n_kv * HEAD_DIM, n_cols=n_kv * HEAD_DIM, out_dtype=BF16, name="win_v")
            sink = win_sink[0]
            o = window_attention(q, k, v, sink, n_heads_kv=n_kv, G=grp, n_lat=n_lat, n_ctx=n_ctx, out_rows=rows)
            if need_ctx:
                o = flash_attention(q, k, v, sink=sink, n_heads_kv=n_kv, G=grp, dq=HEAD_DIM, dv=HEAD_DIM,
                                    k_col=lambda hh: hh, v_col=lambda hh: hh, out=o, name="win_ctx", **ctx_kw)
            w_o = win_w_o[0]
        elif kind == 1:
            w = qkn_w_qkv[0]
            scale = HEAD_DIM ** -0.5
            q = project_heads(h, w, tab_h, qkn_q_gain[0].reshape(1, HEAD_DIM), rows=rows, col0=0, n_cols=d, half=32,
                              scale=scale, norm=True, name="qkn_q")
            k = project_heads(h, w, tab_h, qkn_k_gain[0].reshape(1, HEAD_DIM), rows=t, col0=d,
                              n_cols=n_kv * HEAD_DIM, half=32, scale=1.0, norm=True, name="qkn_k")
            v = matmul(h, w, rows=t, col0=d + n_kv * HEAD_DIM, n_cols=n_kv * HEAD_DIM, out_dtype=BF16, name="qkn_v")
            kw = dict(n_heads_kv=n_kv, G=grp, dq=HEAD_DIM, dv=HEAD_DIM, k_col=lambda hh: hh, v_col=lambda hh: hh)
            o = flash_attention(q, k, v, name="qkn_lat", **kw, **lat_kw)
            if need_ctx:
                o = flash_attention(q, k, v, out=o, name="qkn_ctx", **kw, **ctx_kw)
            w_o = qkn_w_o[0]
        elif kind == 2:
            w_a = mla_w_a[0]
            q_rank = mla_q_gain.shape[1]
            kv_rank = mla_kv_gain.shape[1]
            scale = (C_NOPE + C_ROPE) ** -0.5
            cq = matmul(h, w_a, rows=rows, col0=0, n_cols=q_rank, out_dtype=F32, name="mla_cq")
            ckv = matmul(h, w_a, rows=t, col0=q_rank, n_cols=kv_rank, out_dtype=F32, name="mla_ckv")
            w_pe = jnp.pad(w_a[:, q_rank + kv_rank:], ((0, 0), (0, LANES - C_ROPE)))
            k_pe = project_heads(h, w_pe, tab_r, None, rows=t, col0=0, n_cols=LANES, half=16, scale=1.0,
                                 name="mla_kpe")
            w_qb = mla_w_qb[0].reshape(q_rank, n_heads, C_NOPE + C_ROPE)
            w_qb = jnp.pad(w_qb, ((0, 0), (0, 0), (0, 2 * LANES - C_NOPE - C_ROPE))).reshape(q_rank, n_heads * 2 * LANES)
            tm = _pick(rows, (1024, 768, 512, 256, 128))
            q_epi = functools.partial(_head_epilogue, half=16, scale=scale, norm=False, rope_chunks=(1, 2))
            q = rms_matmul(cq, mla_q_gain[0], w_qb, rows=rows, n_cols=n_heads * 2 * LANES, out_dtype=BF16,
                           epilogue=lambda acc, cs, sn: q_epi(acc, cs, sn, None),
                           extras=tab_r,
                           extra_specs=(pl.BlockSpec((tm, LANES), lambda j, i: (i, 0)),
                                        pl.BlockSpec((tm, LANES), lambda j, i: (i, 0))), name="mla_q")
            kv = rms_matmul(ckv, mla_kv_gain[0], mla_w_kvb[0], rows=t, n_cols=n_heads * (C_NOPE + C_V),
                            out_dtype=BF16, name="mla_kv")
            kw = dict(k2=k_pe, n_heads_kv=n_heads, G=1, dq=2 * LANES, dv=C_V, k_col=lambda hh: 2 * hh,
                      v_col=lambda hh: 2 * hh + 1)
            o = flash_attention(q, kv, kv, name="mla_lat", **kw, **lat_kw)
            if need_ctx:
                o = flash_attention(q, kv, kv, out=o, name="mla_ctx", **kw, **ctx_kw)
            w_o = mla_w_o[0]
        else:
            w = diff_w_qkv[0]
            lambda_init = 0.8 - 0.6 * math.exp(-0.3 * i)
            scale = DF_HEAD ** -0.5
            q = project_heads(h, w, tab_r, None, rows=rows, col0=0, n_cols=d, half=16, scale=scale, name="diff_q")
            k = project_heads(h, w, tab_r, None, rows=t, col0=d, n_cols=d, half=16, scale=1.0, name="diff_k")
            v = matmul(h, w, rows=t, col0=2 * d, n_cols=d, out_dtype=BF16, name="diff_v")
            kw = dict(n_heads=d // (2 * DF_HEAD), lambda_init=lambda_init)
            o = diff_attention(q, k, v, diff_lambda[0], diff_subln[0], name="diff_lat", **kw, **lat_kw)
            if need_ctx:
                o = diff_attention(q, k, v, diff_lambda[0], diff_subln[0], out=o, name="diff_ctx", **kw, **ctx_kw)
            w_o = diff_w_o[0]

        a = matmul(o, w_o, rows=rows, col0=0, n_cols=d, out_dtype=F32, name="attn_out")
        xs, h2 = ln_mod(xs, a, _vec_pack(mods[i], 2, ln_g[i, 0], ln_b[i, 0], mods[i], 3, 4), n_lat, rows)

        w_r = jnp.concatenate([moe_w_expert[i], moe_w_group[i],
                               jnp.zeros((d, LANES - N_EXPERTS - N_GROUPS), F32)], axis=1)
        b_r = jnp.concatenate([moe_b_expert[i], moe_b_group[i],
                               jnp.zeros((LANES - N_EXPERTS - N_GROUPS,), F32)]).reshape(1, LANES)
        comb = moe_router(h2, w_r, b_r, rows)
        y = moe_dense(h2, comb, moe_w_gate_up, moe_w_down, i, rows)
        if i + 1 < DEPTH:
            xs, h = ln_mod(xs, y, _vec_pack(mods[i], 5, ln_g[i, 1], ln_b[i, 1], mods[i + 1], 0, 1), n_lat, rows)
        else:
            (xs,) = ln_mod(xs, y, _vec_pack(mods[i], 5, ln_g[i, 1], ln_b[i, 1], None, 0, 1), n_lat, rows,
                           emit_h=False)
    return xs[:n_lat].reshape(b, n_lat, d)
```

```python
import functools
import math

import jax
import jax.numpy as jnp
from jax import lax
from jax.experimental import pallas as pl
from jax.experimental.pallas import tpu as pltpu

F32 = jnp.float32
BF16 = jnp.bfloat16

DEPTH = 4
GRID_W = 64
HEAD_DIM = 128
ROPE_BASE = 10000.0
EPS = 1e-6
NEG_INF = -1e30
N_MOD = 6
WINDOW = 128
C_NOPE = 128
C_ROPE = 64
C_V = 128
DF_HEAD = 64
N_GROUPS = 4
EXPERTS_PER_GROUP = 6
N_EXPERTS = N_GROUPS * EXPERTS_PER_GROUP
ALPHA = (2.0 * DEPTH) ** 0.25
LANES = 128
QUERY_COLS = 1024
ONES_ROWS = 16
VMEM_LIMIT_BYTES = 56 * 1024 * 1024


def _pick(n, cands):
    for c in cands:
        if n % c == 0:
            return c
    raise ValueError(f"no tile in {cands} divides {n}")


def _params(*sem):
    return pltpu.CompilerParams(dimension_semantics=sem, vmem_limit_bytes=VMEM_LIMIT_BYTES)


def _mod_kernel(cond_ref, w_ref, b_ref, o_ref):
    a = cond_ref[...]
    a = (a * jax.nn.sigmoid(a)).astype(BF16)
    o_ref[0] = jnp.dot(a, w_ref[0].astype(BF16), preferred_element_type=F32) + b_ref[0]


def modulation_all(cond8, ada_w, ada_b):
    depth, d, n = ada_w.shape
    tn = _pick(n, (512, 256, 128))
    return pl.pallas_call(
        _mod_kernel,
        grid=(depth, n // tn),
        in_specs=[pl.BlockSpec((8, d), lambda l, j: (0, 0)),
                  pl.BlockSpec((1, d, tn), lambda l, j: (l, 0, j)),
                  pl.BlockSpec((1, 1, tn), lambda l, j: (l, 0, j))],
        out_specs=pl.BlockSpec((1, 8, tn), lambda l, j: (l, 0, j)),
        out_shape=jax.ShapeDtypeStruct((depth, 8, n), F32),
        compiler_params=_params("parallel", "parallel"),
        name="modulation",
    )(cond8, ada_w, ada_b.reshape(depth, 1, n))


def _modulate_kernel(x_ref, v_ref, h_ref):
    v = v_ref[0]
    h_ref[...] = (x_ref[...] * (1.0 + v[4:5]) + v[3:4]).astype(h_ref.dtype)


def _ln_mod_kernel(x_ref, y_ref, v_ref, xo_ref, *h_refs):
    v = v_ref[0]
    z = ALPHA * x_ref[...] + v[0:1] * y_ref[...].astype(F32)
    mu = jnp.mean(z, axis=-1, keepdims=True)
    zc = z - mu
    var = jnp.mean(zc * zc, axis=-1, keepdims=True)
    xn = zc * lax.rsqrt(var + EPS) * v[1:2] + v[2:3]
    xo_ref[...] = xn
    if h_refs:
        h_refs[0][...] = (xn * (1.0 + v[4:5]) + v[3:4]).astype(h_refs[0].dtype)


def _row_specs(tr, d, n_lat_tiles):
    row = pl.BlockSpec((tr, d), lambda i: (i, 0))
    vec = pl.BlockSpec((1, 8, d), lambda i: ((i >= n_lat_tiles).astype(jnp.int32), 0, 0))
    return row, vec


def modulate(x, vecs, n_lat):
    t, d = x.shape
    tr = _pick(math.gcd(t, n_lat), (256, 128, 64, 32, 16, 8))
    row, vec = _row_specs(tr, d, n_lat // tr)
    return pl.pallas_call(
        _modulate_kernel, grid=(t // tr,), in_specs=[row, vec], out_specs=row,
        out_shape=jax.ShapeDtypeStruct((t, d), BF16),
        compiler_params=_params("parallel"), name="modulate",
    )(x, vecs)


def ln_mod(x, y, vecs, n_lat, rows, emit_h=True):
    d = x.shape[1]
    tr = _pick(math.gcd(rows, n_lat), (256, 128, 64, 32, 16, 8))
    row, vec = _row_specs(tr, d, n_lat // tr)
    out_shape = [jax.ShapeDtypeStruct((rows, d), F32)]
    out_specs = [row]
    if emit_h:
        out_shape.append(jax.ShapeDtypeStruct((rows, d), BF16))
        out_specs.append(row)
    return pl.pallas_call(
        _ln_mod_kernel, grid=(rows // tr,), in_specs=[row, row, vec], out_specs=out_specs,
        out_shape=out_shape, compiler_params=_params("parallel"), name="ln_mod",
    )(x, y, vecs)


def _mm_kernel(*refs, n_extra, prologue, epilogue, transpose_out):
    a_ref, w_ref = refs[0], refs[1]
    extras = refs[2:2 + n_extra]
    o_ref, wb_ref = refs[2 + n_extra], refs[3 + n_extra]

    @pl.when(pl.program_id(1) == 0)
    def _():
        wb_ref[...] = w_ref[...].astype(BF16)

    a = a_ref[...]
    if prologue is not None:
        a = prologue(a, *extras)
    acc = jnp.dot(a.astype(BF16), wb_ref[...], preferred_element_type=F32)
    if epilogue is not None:
        acc = epilogue(acc, *extras)
    if transpose_out:
        acc = acc.T
    o_ref[...] = acc.astype(o_ref.dtype)


def matmul(a, w, *, rows, col0, n_cols, out_dtype, prologue=None, epilogue=None, extras=(),
           extra_specs=(), out_mult=1, transpose_out=False, name="matmul"):
    k = a.shape[1]
    tm = _pick(rows, (1024, 768, 512, 256, 128))
    tn = _pick(math.gcd(n_cols, col0) if col0 else n_cols, (512, 256, 128))
    cb0 = col0 // tn
    kern = functools.partial(_mm_kernel, n_extra=len(extras), prologue=prologue, epilogue=epilogue,
                             transpose_out=transpose_out)
    if transpose_out:
        out_spec = pl.BlockSpec((tn * out_mult, tm), lambda j, i: (j, i))
        out_shape = jax.ShapeDtypeStruct((n_cols * out_mult, rows), out_dtype)
    else:
        out_spec = pl.BlockSpec((tm, tn * out_mult), lambda j, i: (i, j))
        out_shape = jax.ShapeDtypeStruct((rows, n_cols * out_mult), out_dtype)
    return pl.pallas_call(
        kern,
        grid=(n_cols // tn, rows // tm),
        in_specs=[pl.BlockSpec((tm, k), lambda j, i: (i, 0)),
                  pl.BlockSpec((k, tn), lambda j, i: (0, cb0 + j)),
                  *extra_specs],
        out_specs=out_spec,
        out_shape=out_shape,
        scratch_shapes=[pltpu.VMEM((k, tn), BF16)],
        compiler_params=_params("parallel", "arbitrary"),
        name=name,
    )(a, w, *extras)


def _swap_pairs(x, half):
    n = x.shape[-1]
    lane = lax.broadcasted_iota(jnp.int32, x.shape, x.ndim - 1)
    fwd = pltpu.roll(x, n - half, axis=x.ndim - 1)
    bwd = pltpu.roll(x, half, axis=x.ndim - 1)
    return jnp.where((lane % (2 * half)) < half, fwd, bwd)


def _rope(x, cos, sin_signed, half):
    return x * cos + _swap_pairs(x, half) * sin_signed


def _rms(x, gain):
    return x * lax.rsqrt(jnp.mean(x * x, axis=-1, keepdims=True) + EPS) * gain


def _head_epilogue(acc, cos_ref, sin_ref, gain_ref, *, half, scale, norm, rope_chunks, split_halves):
    outs = []
    for c in range(acc.shape[1] // LANES):
        x = acc[:, c * LANES:(c + 1) * LANES]
        if norm:
            x = _rms(x, gain_ref[...])
        if rope_chunks is None or (c % rope_chunks[1]) == rope_chunks[0]:
            x = _rope(x, cos_ref[...], sin_ref[...], half)
        if scale != 1.0:
            x = x * scale
        if split_halves:
            lane = lax.broadcasted_iota(jnp.int32, x.shape, 1)
            outs.append(jnp.where(lane < LANES // 2, x, 0.0))
            outs.append(jnp.where(lane >= LANES // 2, x, 0.0))
        else:
            outs.append(x)
    return jnp.concatenate(outs, axis=1) if len(outs) > 1 else outs[0]


def project_heads(a, w, tables, gain, *, rows, col0, n_cols, half, scale, norm=False,
                  rope_chunks=None, split_halves=False, name="proj"):
    cos, sin = tables
    tm = _pick(rows, (1024, 768, 512, 256, 128))
    if gain is None:
        gain = jnp.ones((1, LANES), F32)
    epi = functools.partial(_head_epilogue, half=half, scale=scale, norm=norm, rope_chunks=rope_chunks,
                            split_halves=split_halves)
    return matmul(
        a, w, rows=rows, col0=col0, n_cols=n_cols, out_dtype=BF16, epilogue=epi,
        extras=(cos, sin, gain),
        extra_specs=(pl.BlockSpec((tm, LANES), lambda j, i: (i, 0)),
                     pl.BlockSpec((tm, LANES), lambda j, i: (i, 0)),
                     pl.BlockSpec((1, LANES), lambda j, i: (0, 0))),
        out_mult=2 if split_halves else 1, name=name)


def rms_matmul(a, gain, w, *, rows, n_cols, out_dtype, epilogue=None, extras=(), extra_specs=(),
               transpose_out=False, name="rms_mm"):
    k = a.shape[1]
    pro = lambda a_t, g_ref, *rest: _rms(a_t, g_ref[...])
    epi = None if epilogue is None else (lambda acc, g_ref, *rest: epilogue(acc, *rest))
    return matmul(a, w, rows=rows, col0=0, n_cols=n_cols, out_dtype=out_dtype, prologue=pro, epilogue=epi,
                  extras=(gain.reshape(1, k), *extras),
                  extra_specs=(pl.BlockSpec((1, k), lambda j, i: (0, 0)), *extra_specs),
                  transpose_out=transpose_out, name=name)


def _kq(k, q):
    return lax.dot_general(k, q, (((1,), (1,)), ((), ())), preferred_element_type=F32)


def _col_reduce(x, op):
    rows, n = x.shape
    parts = 8 if rows % 64 == 0 else 1
    if parts > 1:
        x = op(x.reshape(parts, rows // parts, n), axis=1)
    return op(x, axis=0, keepdims=True)


def _flash_kernel(*refs, n_kv, G, dq, has_k2, diff, nq_lat, nk, kc, qcols):
    refs = list(refs)
    if diff is not None:
        lam_ref, sub_ref = refs.pop(0), refs.pop(0)
    q_ref, k_ref = refs.pop(0), refs.pop(0)
    k2_ref = refs.pop(0) if has_k2 else None
    vt_ref, kx_ref = refs.pop(0), refs.pop(0)
    kx2_ref = refs.pop(0) if has_k2 else None
    vxt_ref = refs.pop(0)
    o_ref, m_scr, acc_scr = refs
    i, j = pl.program_id(1), pl.program_id(2)
    tq = q_ref.shape[0]

    def folded_q(s):
        parts = [q_ref[:, (s * G + g) * dq:(s * G + g + 1) * dq] for g in range(G)]
        return jnp.concatenate(parts, axis=0) if G > 1 else parts[0]

    def scores(qs, k_r, k2_r, s, rows, cols):
        k = k_r[rows, s * LANES:(s + 1) * LANES]
        if k2_r is not None:
            k = jnp.concatenate([k, k2_r[rows, :]], axis=1)
        return _kq(k, qs[s][cols])

    def update(s, cols, st, vt):
        m_prev = m_scr[s, :, cols]
        m_new = jnp.maximum(m_prev, _col_reduce(st, jnp.max))
        p = jnp.exp(st - m_new).astype(BF16)
        alpha = jnp.exp(m_prev - m_new)
        vt1 = jnp.concatenate([vt, jnp.ones((ONES_ROWS, vt.shape[1]), BF16)], axis=0)
        acc_scr[s, :, cols] = acc_scr[s, :, cols] * alpha + jnp.dot(vt1, p, preferred_element_type=F32)
        m_scr[s, :, cols] = m_new

    def run(k_r, k2_r, vt_r, row_slices):
        qs = [folded_q(s) for s in range(n_kv)]
        col_slices = [slice(c0, c0 + qcols) for c0 in range(0, G * tq, qcols)]
        items = [(s, rows, cols) for rows in row_slices for s in range(n_kv) for cols in col_slices]
        st_next = scores(qs, k_r, k2_r, *items[0])
        for t, (s, rows, cols) in enumerate(items):
            st = st_next
            if t + 1 < len(items):
                st_next = scores(qs, k_r, k2_r, *items[t + 1])
            update(s, cols, st, vt_r[s * LANES:(s + 1) * LANES, rows])

    @pl.when(j == 0)
    def _():
        m_scr[...] = jnp.full(m_scr.shape, NEG_INF, F32)
        acc_scr[...] = jnp.zeros(acc_scr.shape, F32)
        run(kx_ref, kx2_ref, vxt_ref, [slice(None)])

    @pl.when(i < nq_lat)
    def _():
        run(k_ref, k2_ref, vt_ref, [slice(cc * kc, (cc + 1) * kc) for cc in range(k_ref.shape[0] // kc)])

    def normalized(s):
        return acc_scr[s, :LANES] / acc_scr[s, LANES:LANES + 1]

    @pl.when(j == nk - 1)
    def _():
        if diff is None:
            for s in range(n_kv):
                o = normalized(s)
                for g in range(G):
                    c = s * G + g
                    o_ref[:, c * LANES:(c + 1) * LANES] = o[:, g * tq:(g + 1) * tq].T.astype(o_ref.dtype)
        else:
            lv = lam_ref[...]
            lam = (jnp.exp(jnp.sum(lv[0:1] * lv[1:2], axis=1, keepdims=True))
                   - jnp.exp(jnp.sum(lv[2:3] * lv[3:4], axis=1, keepdims=True)) + diff)
            for s in range(n_kv):
                o = normalized(s)
                o = o[:, :tq] - lam * o[:, tq:]
                o = _rms(o.T, sub_ref[...]) * (1.0 - diff)
                o_ref[:, s * LANES:(s + 1) * LANES] = o.astype(o_ref.dtype)


def flash_attention(q, k, vt, *, k2=None, diff=None, n_groups, n_kv, G, dq, n_lat, n_ctx,
                    with_ctx_q, name):
    tq = n_ctx
    nq_lat = n_lat // tq
    nqb = nq_lat + (1 if with_ctx_q else 0)
    tk = _pick(n_lat, (1024, 512, 256, 128))
    nk = n_lat // tk
    kc = min(tk, 512)
    cb = n_lat // n_ctx
    kw = n_kv * LANES
    n_slots = n_kv * G
    ow = (n_kv if diff is not None else n_slots) * LANES

    def jmap(i, j):
        return jnp.where(i < nq_lat, j, 0)

    args, specs = [], []
    if diff is not None:
        lam_vecs, subln, lambda_init = diff
        args += [lam_vecs, subln.reshape(1, LANES)]
        specs += [pl.BlockSpec(lam_vecs.shape, lambda h, i, j: (0, 0)),
                  pl.BlockSpec((1, LANES), lambda h, i, j: (0, 0))]
    args += [q, k]
    specs += [pl.BlockSpec((tq, n_slots * dq), lambda h, i, j: (i, h)),
              pl.BlockSpec((tk, kw), lambda h, i, j: (jmap(i, j), h))]
    if k2 is not None:
        args.append(k2)
        specs.append(pl.BlockSpec((tk, LANES), lambda h, i, j: (jmap(i, j), 0)))
    args += [vt, k]
    specs += [pl.BlockSpec((kw, tk), lambda h, i, j: (h, jmap(i, j))),
              pl.BlockSpec((n_ctx, kw), lambda h, i, j: (cb, h))]
    if k2 is not None:
        args.append(k2)
        specs.append(pl.BlockSpec((n_ctx, LANES), lambda h, i, j: (cb, 0)))
    args.append(vt)
    specs.append(pl.BlockSpec((kw, n_ctx), lambda h, i, j: (h, cb)))
    kern = functools.partial(_flash_kernel, n_kv=n_kv, G=G, dq=dq, has_k2=k2 is not None,
                             diff=None if diff is None else diff[2],
                             nq_lat=nq_lat, nk=nk, kc=kc, qcols=min(G * tq, QUERY_COLS))
    return pl.pallas_call(
        kern,
        grid=(n_groups, nqb, nk),
        in_specs=specs,
        out_specs=pl.BlockSpec((tq, ow), lambda h, i, j: (i, h)),
        out_shape=jax.ShapeDtypeStruct((nqb * tq, n_groups * ow), BF16),
        scratch_shapes=[pltpu.VMEM((n_kv, 1, G * tq), F32),
                        pltpu.VMEM((n_kv, LANES + ONES_ROWS, G * tq), F32)],
        compiler_params=_params("parallel", "parallel", "arbitrary"),
        name=name,
    )(*args)


def _qk(q, k):
    return lax.dot_general(q, k, (((1,), (1,)), ((), ())), preferred_element_type=F32)


def _window_kernel(sink_ref, q_ref, kp_ref, kc_ref, kn_ref, kx_ref, vp_ref, vc_ref, vn_ref, vx_ref,
                   o_ref, *, G, tq, nq):
    h, i = pl.program_id(0), pl.program_id(1)
    k = jnp.concatenate([kp_ref[...], kc_ref[...], kn_ref[...], kx_ref[...]], axis=0)
    v = jnp.concatenate([vp_ref[...], vc_ref[...], vn_ref[...], vx_ref[...]], axis=0)
    n_band = tq + 2 * WINDOW
    r = lax.broadcasted_iota(jnp.int32, (tq, k.shape[0]), 0)
    c = lax.broadcasted_iota(jnp.int32, (tq, k.shape[0]), 1)
    rel = c - WINDOW - r
    kpos = i * tq - WINDOW + c
    band = (jnp.abs(rel) <= WINDOW) & (kpos >= 0) & (kpos < nq * tq) & (i * tq + r < nq * tq)
    mask = band | (c >= n_band)
    for g in range(G):
        s = _qk(q_ref[:, g * HEAD_DIM:(g + 1) * HEAD_DIM], k)
        s = jnp.where(mask, s, NEG_INF)
        sk = sink_ref[h * G + g]
        m = jnp.maximum(jnp.max(s, axis=1, keepdims=True), sk)
        p = jnp.exp(s - m)
        l = jnp.sum(p, axis=1, keepdims=True) + jnp.exp(sk - m)
        o = jnp.dot(p.astype(BF16), v, preferred_element_type=F32) / l
        o_ref[:, g * HEAD_DIM:(g + 1) * HEAD_DIM] = o.astype(o_ref.dtype)


def window_attention(q, k, v, sink, *, n_heads_kv, G, n_lat, n_ctx, with_ctx_q):
    tq = 2 * WINDOW
    assert n_ctx == tq
    nq = n_lat // tq
    nqb = nq + (1 if with_ctx_q else 0)
    nb = n_lat // WINDOW
    cb = n_lat // n_ctx
    w = WINDOW

    def prev_map(h, i):
        return (jnp.maximum(2 * i - 1, 0), h)

    def next_map(h, i):
        return (jnp.minimum(2 * i + 2, nb - 1), h)

    kv_specs = [pl.BlockSpec((w, HEAD_DIM), prev_map),
                pl.BlockSpec((tq, HEAD_DIM), lambda h, i: (i, h)),
                pl.BlockSpec((w, HEAD_DIM), next_map),
                pl.BlockSpec((n_ctx, HEAD_DIM), lambda h, i: (cb, h))]
    kern = functools.partial(_window_kernel, G=G, tq=tq, nq=nq)
    return pl.pallas_call(
        kern,
        grid=(n_heads_kv, nqb),
        in_specs=[pl.BlockSpec(memory_space=pltpu.SMEM),
                  pl.BlockSpec((tq, G * HEAD_DIM), lambda h, i: (i, h)),
                  *kv_specs, *kv_specs],
        out_specs=pl.BlockSpec((tq, G * HEAD_DIM), lambda h, i: (i, h)),
        out_shape=jax.ShapeDtypeStruct((nqb * tq, n_heads_kv * G * HEAD_DIM), BF16),
        compiler_params=_params("parallel", "parallel"),
        name="window_attention",
    )(sink, q, k, k, k, k, v, v, v, v)


def _router_kernel(h_ref, w_ref, b_ref, comb_ref):
    logits = jnp.dot(h_ref[...], w_ref[...].astype(BF16), preferred_element_type=F32) + b_ref[...]
    lane = lax.broadcasted_iota(jnp.int32, logits.shape, 1)
    big = jnp.int32(1 << 20)
    is_group = (lane >= N_EXPERTS) & (lane < N_EXPERTS + N_GROUPS)
    gl = jnp.where(is_group, logits, NEG_INF)
    gmax = jnp.max(gl, axis=1, keepdims=True)
    g_idx = jnp.min(jnp.where(gl == gmax, lane, big), axis=1, keepdims=True) - N_EXPERTS
    g_w = 1.0 / jnp.sum(jnp.exp(gl - gmax), axis=1, keepdims=True)
    lo = g_idx * EXPERTS_PER_GROUP
    el = jnp.where((lane >= lo) & (lane < lo + EXPERTS_PER_GROUP), logits, NEG_INF)
    v1 = jnp.max(el, axis=1, keepdims=True)
    i1 = jnp.min(jnp.where(el == v1, lane, big), axis=1, keepdims=True)
    el2 = jnp.where(lane == i1, NEG_INF, el)
    v2 = jnp.max(el2, axis=1, keepdims=True)
    i2 = jnp.min(jnp.where(el2 == v2, lane, big), axis=1, keepdims=True)
    e2 = jnp.exp(v2 - v1)
    den = 1.0 + e2
    w1 = (1.0 / den) * g_w
    w2 = (e2 / den) * g_w
    comb_ref[...] = jnp.where(lane == i1, w1, 0.0) + jnp.where(lane == i2, w2, 0.0)


def moe_router(h, w_r, b_r, rows):
    d = h.shape[1]
    tm = _pick(rows, (768, 512, 256, 128))
    return pl.pallas_call(
        _router_kernel, grid=(rows // tm,),
        in_specs=[pl.BlockSpec((tm, d), lambda i: (i, 0)),
                  pl.BlockSpec((d, LANES), lambda i: (0, 0)),
                  pl.BlockSpec((1, LANES), lambda i: (0, 0))],
        out_specs=pl.BlockSpec((tm, LANES), lambda i: (i, 0)),
        out_shape=jax.ShapeDtypeStruct((rows, LANES), F32),
        compiler_params=_params("parallel"), name="moe_router",
    )(h, w_r, b_r)


def _moe_dense_kernel(h_ref, comb_ref, wgu_ref, wd_ref, o_ref):
    e = pl.program_id(1)

    @pl.when(e == 0)
    def _():
        o_ref[...] = jnp.zeros(o_ref.shape, o_ref.dtype)

    gu = jnp.dot(h_ref[...], wgu_ref[0, 0].astype(BF16), preferred_element_type=F32)
    f = gu.shape[1] // 2
    gate, up = gu[:, :f], gu[:, f:]
    comb = comb_ref[...]
    lane = lax.broadcasted_iota(jnp.int32, comb.shape, 1)
    c = jnp.sum(jnp.where(lane == e, comb, 0.0), axis=1, keepdims=True)
    act = (gate * jax.nn.sigmoid(gate)) * up * c
    o_ref[...] += jnp.dot(act.astype(BF16), wd_ref[0, 0].astype(BF16), preferred_element_type=F32)


def moe_dense(h, comb, w_gate_up, w_down, layer, rows):
    d = h.shape[1]
    n_e, f2 = w_gate_up.shape[1], w_gate_up.shape[3]
    tm = _pick(rows, (384, 256, 128))
    return pl.pallas_call(
        _moe_dense_kernel, grid=(rows // tm, n_e),
        in_specs=[pl.BlockSpec((tm, d), lambda i, e: (i, 0)),
                  pl.BlockSpec((tm, LANES), lambda i, e: (i, 0)),
                  pl.BlockSpec((1, 1, d, f2), lambda i, e: (layer, e, 0, 0)),
                  pl.BlockSpec((1, 1, f2 // 2, d), lambda i, e: (layer, e, 0, 0))],
        out_specs=pl.BlockSpec((tm, d), lambda i, e: (i, 0)),
        out_shape=jax.ShapeDtypeStruct((rows, d), F32),
        compiler_params=_params("parallel", "arbitrary"), name="moe_dense",
    )(h, comb, w_gate_up, w_down)


def _rope_tables(n_lat, n_ctx, rot_dim):
    rows = n_lat // GRID_W
    r, col = jnp.meshgrid(jnp.arange(rows, dtype=F32), jnp.arange(GRID_W, dtype=F32), indexing="ij")
    pos = jnp.stack([r.reshape(-1), col.reshape(-1)], axis=-1)
    n_freq = rot_dim // 4
    inv_freq = ROPE_BASE ** (-jnp.arange(n_freq, dtype=F32) / n_freq)
    ang = pos[:, :, None] * inv_freq
    cos, sin = jnp.cos(ang), jnp.sin(ang)
    cos_full = jnp.concatenate([cos[:, 0], cos[:, 0], cos[:, 1], cos[:, 1]], axis=-1)
    sin_full = jnp.concatenate([-sin[:, 0], sin[:, 0], -sin[:, 1], sin[:, 1]], axis=-1)
    reps = LANES // rot_dim
    cos_full = jnp.tile(cos_full, (1, reps))
    sin_full = jnp.tile(sin_full, (1, reps))
    cos_full = jnp.concatenate([cos_full, jnp.ones((n_ctx, LANES), F32)], axis=0)
    sin_full = jnp.concatenate([sin_full, jnp.zeros((n_ctx, LANES), F32)], axis=0)
    return cos_full, sin_full


def _vec_pack(mods, gate_idx, ln_g, ln_b, next_mods, shift_idx, scale_idx):
    d = mods.shape[-1]
    z = jnp.zeros((2, d), F32)
    gate = mods[:2, gate_idx] if gate_idx is not None else z
    g = jnp.broadcast_to(ln_g, (2, d)) if ln_g is not None else z
    b = jnp.broadcast_to(ln_b, (2, d)) if ln_b is not None else z
    shift = next_mods[:2, shift_idx] if next_mods is not None else z
    scale = next_mods[:2, scale_idx] if next_mods is not None else z
    return jnp.stack([gate, g, b, shift, scale, z, z, z], axis=1)


def kernel(x, c, ctx, c_ctx, ada_w, ada_b, ln_g, ln_b, win_w_qkv, win_w_o, win_sink, qkn_w_qkv, qkn_q_gain, qkn_k_gain, qkn_w_o, mla_w_a, mla_q_gain, mla_kv_gain, mla_w_qb, mla_w_kvb, mla_w_o, diff_w_qkv, diff_lambda, diff_subln, diff_w_o, moe_w_group, moe_b_group, moe_w_expert, moe_b_expert, moe_w_gate_up, moe_w_down):
    b, n_lat, d = x.shape
    n_ctx = ctx.shape[1]
    assert b == 1
    t = n_lat + n_ctx
    n_heads = d // HEAD_DIM
    n_kv = n_heads // 4
    grp = n_heads // n_kv

    tab_h = _rope_tables(n_lat, n_ctx, HEAD_DIM)
    tab_r = _rope_tables(n_lat, n_ctx, C_ROPE)

    cond8 = jnp.zeros((8, d), F32).at[0].set(c[0]).at[1].set(c_ctx)
    mods = modulation_all(cond8, ada_w, ada_b).reshape(DEPTH, 8, N_MOD, d)

    xs = jnp.concatenate([x[0], ctx[0]], axis=0)
    h = modulate(xs, _vec_pack(mods[0], None, None, None, mods[0], 0, 1), n_lat)

    for i in range(DEPTH):
        kind = i % 4
        need_ctx = i < DEPTH - 1
        rows = t if need_ctx else n_lat
        att_kw = dict(n_lat=n_lat, n_ctx=n_ctx, with_ctx_q=need_ctx)
        if kind == 0:
            w = win_w_qkv[0]
            scale = HEAD_DIM ** -0.5
            q = project_heads(h, w, tab_h, None, rows=rows, col0=0, n_cols=d, half=32, scale=scale, name="win_q")
            k = project_heads(h, w, tab_h, None, rows=t, col0=d, n_cols=n_kv * HEAD_DIM, half=32, scale=1.0,
                              name="win_k")
            v = matmul(h, w, rows=t, col0=d + n_kv * HEAD_DIM, n_cols=n_kv * HEAD_DIM, out_dtype=BF16, name="win_v")
            o = window_attention(q, k, v, win_sink[0], n_heads_kv=n_kv, G=grp, **att_kw)
            w_o = win_w_o[0]
        elif kind == 1:
            w = qkn_w_qkv[0]
            scale = HEAD_DIM ** -0.5
            q = project_heads(h, w, tab_h, qkn_q_gain[0].reshape(1, HEAD_DIM), rows=rows, col0=0, n_cols=d, half=32,
                              scale=scale, norm=True, name="qkn_q")
            k = project_heads(h, w, tab_h, qkn_k_gain[0].reshape(1, HEAD_DIM), rows=t, col0=d,
                              n_cols=n_kv * HEAD_DIM, half=32, scale=1.0, norm=True, name="qkn_k")
            vt = matmul(h, w, rows=t, col0=d + n_kv * HEAD_DIM, n_cols=n_kv * HEAD_DIM, out_dtype=BF16,
                        transpose_out=True, name="qkn_vt")
            o = flash_attention(q, k, vt, n_groups=n_kv, n_kv=1, G=grp, dq=HEAD_DIM, name="qkn_attn", **att_kw)
            w_o = qkn_w_o[0]
        elif kind == 2:
            w_a = mla_w_a[0]
            q_rank = mla_q_gain.shape[1]
            kv_rank = mla_kv_gain.shape[1]
            scale = (C_NOPE + C_ROPE) ** -0.5
            cq = matmul(h, w_a, rows=rows, col0=0, n_cols=q_rank, out_dtype=F32, name="mla_cq")
            ckv = matmul(h, w_a, rows=t, col0=q_rank, n_cols=kv_rank, out_dtype=F32, name="mla_ckv")
            w_pe = jnp.pad(w_a[:, q_rank + kv_rank:], ((0, 0), (0, LANES - C_ROPE)))
            k_pe = project_heads(h, w_pe, tab_r, None, rows=t, col0=0, n_cols=LANES, half=16, scale=1.0,
                                 name="mla_kpe")
            w_qb = mla_w_qb[0].reshape(q_rank, n_heads, C_NOPE + C_ROPE)
            w_qb = jnp.pad(w_qb, ((0, 0), (0, 0), (0, 2 * LANES - C_NOPE - C_ROPE))).reshape(q_rank, n_heads * 2 * LANES)
            w_kvb = mla_w_kvb[0].reshape(kv_rank, n_heads, C_NOPE + C_V)
            w_kn = w_kvb[:, :, :C_NOPE].reshape(kv_rank, n_heads * C_NOPE)
            w_v = w_kvb[:, :, C_NOPE:].reshape(kv_rank, n_heads * C_V)
            tm = _pick(rows, (1024, 768, 512, 256, 128))
            q_epi = functools.partial(_head_epilogue, half=16, scale=scale, norm=False, rope_chunks=(1, 2),
                                      split_halves=False)
            q = rms_matmul(cq, mla_q_gain[0], w_qb, rows=rows, n_cols=n_heads * 2 * LANES, out_dtype=BF16,
                           epilogue=lambda acc, cs, sn: q_epi(acc, cs, sn, None),
                           extras=tab_r,
                           extra_specs=(pl.BlockSpec((tm, LANES), lambda j, i: (i, 0)),
                                        pl.BlockSpec((tm, LANES), lambda j, i: (i, 0))), name="mla_q")
            k_nope = rms_matmul(ckv, mla_kv_gain[0], w_kn, rows=t, n_cols=n_heads * C_NOPE, out_dtype=BF16,
                                name="mla_kn")
            vt = rms_matmul(ckv, mla_kv_gain[0], w_v, rows=t, n_cols=n_heads * C_V, out_dtype=BF16,
                            transpose_out=True, name="mla_vt")
            hp = 4
            o = flash_attention(q, k_nope, vt, k2=k_pe, n_groups=n_heads // hp, n_kv=hp, G=1, dq=2 * LANES,
                                name="mla_attn", **att_kw)
            w_o = mla_w_o[0]
        else:
            w = diff_w_qkv[0]
            lambda_init = 0.8 - 0.6 * math.exp(-0.3 * i)
            scale = DF_HEAD ** -0.5
            n_dh = d // (2 * DF_HEAD)
            q = project_heads(h, w, tab_r, None, rows=rows, col0=0, n_cols=d, half=16, scale=scale,
                              split_halves=True, name="diff_q")
            k = project_heads(h, w, tab_r, None, rows=t, col0=d, n_cols=d, half=16, scale=1.0, name="diff_k")
            vt = matmul(h, w, rows=t, col0=2 * d, n_cols=d, out_dtype=BF16, transpose_out=True, name="diff_vt")
            hp = 2
            o = flash_attention(q, k, vt, diff=(diff_lambda[0], diff_subln[0], lambda_init), n_groups=n_dh // hp,
                                n_kv=hp, G=2, dq=LANES, name="diff_attn", **att_kw)
            w_o = diff_w_o[0]

        a = matmul(o, w_o, rows=rows, col0=0, n_cols=d, out_dtype=F32, name="attn_out")
        xs, h2 = ln_mod(xs, a, _vec_pack(mods[i], 2, ln_g[i, 0], ln_b[i, 0], mods[i], 3, 4), n_lat, rows)

        w_r = jnp.concatenate([moe_w_expert[i], moe_w_group[i],
                               jnp.zeros((d, LANES - N_EXPERTS - N_GROUPS), F32)], axis=1)
        b_r = jnp.concatenate([moe_b_expert[i], moe_b_group[i],
                               jnp.zeros((LANES - N_EXPERTS - N_GROUPS,), F32)]).reshape(1, LANES)
        comb = moe_router(h2, w_r, b_r, rows)
        y = moe_dense(h2, comb, moe_w_gate_up, moe_w_down, i, rows)
        if i + 1 < DEPTH:
            xs, h = ln_mod(xs, y, _vec_pack(mods[i], 5, ln_g[i, 1], ln_b[i, 1], mods[i + 1], 0, 1), n_lat, rows)
        else:
            (xs,) = ln_mod(xs, y, _vec_pack(mods[i], 5, ln_g[i, 1], ln_b[i, 1], None, 0, 1), n_lat, rows,
                           emit_h=False)
    return xs[:n_lat].reshape(b, n_lat, d)
```

```python
import functools
import math

import jax
import jax.numpy as jnp
from jax import lax
from jax.experimental import pallas as pl
from jax.experimental.pallas import tpu as pltpu

F32 = jnp.float32
BF16 = jnp.bfloat16

DEPTH = 4
GRID_W = 64
HEAD_DIM = 128
ROPE_BASE = 10000.0
EPS = 1e-6
NEG_INF = -1e30
N_MOD = 6
WINDOW = 128
C_NOPE = 128
C_ROPE = 64
C_V = 128
DF_HEAD = 64
N_GROUPS = 4
EXPERTS_PER_GROUP = 6
N_EXPERTS = N_GROUPS * EXPERTS_PER_GROUP
ALPHA = (2.0 * DEPTH) ** 0.25
LANES = 128
MOE_TILE = 256
QUERY_COLS = 1024
ONES_ROWS = 16
VMEM_LIMIT_BYTES = 56 * 1024 * 1024


def _pick(n, cands):
    for c in cands:
        if n % c == 0:
            return c
    raise ValueError(f"no tile in {cands} divides {n}")


def _params(*sem):
    return pltpu.CompilerParams(dimension_semantics=sem, vmem_limit_bytes=VMEM_LIMIT_BYTES)


def _mod_kernel(cond_ref, w_ref, b_ref, o_ref):
    a = cond_ref[...]
    a = (a * jax.nn.sigmoid(a)).astype(BF16)
    o_ref[0] = jnp.dot(a, w_ref[0].astype(BF16), preferred_element_type=F32) + b_ref[0]


def modulation_all(cond8, ada_w, ada_b):
    depth, d, n = ada_w.shape
    tn = _pick(n, (512, 256, 128))
    return pl.pallas_call(
        _mod_kernel,
        grid=(depth, n // tn),
        in_specs=[pl.BlockSpec((8, d), lambda l, j: (0, 0)),
                  pl.BlockSpec((1, d, tn), lambda l, j: (l, 0, j)),
                  pl.BlockSpec((1, 1, tn), lambda l, j: (l, 0, j))],
        out_specs=pl.BlockSpec((1, 8, tn), lambda l, j: (l, 0, j)),
        out_shape=jax.ShapeDtypeStruct((depth, 8, n), F32),
        compiler_params=_params("parallel", "parallel"),
        name="modulation",
    )(cond8, ada_w, ada_b.reshape(depth, 1, n))


def _modulate_kernel(x_ref, v_ref, h_ref):
    v = v_ref[0]
    h_ref[...] = (x_ref[...] * (1.0 + v[4:5]) + v[3:4]).astype(h_ref.dtype)


def _deepnorm(x, y, v):
    z = ALPHA * x + v[0:1] * y
    mu = jnp.mean(z, axis=-1, keepdims=True)
    zc = z - mu
    var = jnp.mean(zc * zc, axis=-1, keepdims=True)
    xn = zc * lax.rsqrt(var + EPS) * v[1:2] + v[2:3]
    return xn, xn * (1.0 + v[4:5]) + v[3:4]


def _route(logits):
    lane = lax.broadcasted_iota(jnp.int32, logits.shape, 1)
    big = jnp.int32(1 << 20)
    is_group = (lane >= N_EXPERTS) & (lane < N_EXPERTS + N_GROUPS)
    gl = jnp.where(is_group, logits, NEG_INF)
    gmax = jnp.max(gl, axis=1, keepdims=True)
    g_idx = jnp.min(jnp.where(gl == gmax, lane, big), axis=1, keepdims=True) - N_EXPERTS
    g_w = 1.0 / jnp.sum(jnp.exp(gl - gmax), axis=1, keepdims=True)
    lo = g_idx * EXPERTS_PER_GROUP
    el = jnp.where((lane >= lo) & (lane < lo + EXPERTS_PER_GROUP), logits, NEG_INF)
    v1 = jnp.max(el, axis=1, keepdims=True)
    i1 = jnp.min(jnp.where(el == v1, lane, big), axis=1, keepdims=True)
    el2 = jnp.where(lane == i1, NEG_INF, el)
    v2 = jnp.max(el2, axis=1, keepdims=True)
    i2 = jnp.min(jnp.where(el2 == v2, lane, big), axis=1, keepdims=True)
    e2 = jnp.exp(v2 - v1)
    den = 1.0 + e2
    w1 = (1.0 / den) * g_w
    w2 = (e2 / den) * g_w
    comb = jnp.where(lane == i1, w1, 0.0) + jnp.where(lane == i2, w2, 0.0)
    comb = jnp.where(lane == N_EXPERTS, i1.astype(F32), comb)
    return jnp.where(lane == N_EXPERTS + 1, i2.astype(F32), comb)


def _ln_router_kernel(x_ref, y_ref, v_ref, wr_ref, br_ref, xo_ref, hx_ref):
    d = x_ref.shape[1]
    xn, h = _deepnorm(x_ref[...], y_ref[...], v_ref[0])
    xo_ref[...] = xn
    logits = jnp.dot(h.astype(BF16), wr_ref[...].astype(BF16), preferred_element_type=F32) + br_ref[...]
    hx_ref[:, :d] = h
    hx_ref[:, d:] = _route(logits)


def ln_router(x, y, vecs, w_r, b_r, n_lat, rows):
    d = x.shape[1]
    tr = _pick(math.gcd(rows, n_lat), (256, 128, 64, 32, 16, 8))
    row, vec = _row_specs(tr, d, n_lat // tr)
    return pl.pallas_call(
        _ln_router_kernel, grid=(rows // tr,),
        in_specs=[row, row, vec, pl.BlockSpec((d, LANES), lambda i: (0, 0)),
                  pl.BlockSpec((1, LANES), lambda i: (0, 0))],
        out_specs=[row, pl.BlockSpec((tr, d + LANES), lambda i: (i, 0))],
        out_shape=[jax.ShapeDtypeStruct((rows, d), F32), jax.ShapeDtypeStruct((rows, d + LANES), F32)],
        compiler_params=_params("parallel"), name="ln_router",
    )(x, y, vecs, w_r, b_r)


def _row_specs(tr, d, n_lat_tiles):
    row = pl.BlockSpec((tr, d), lambda i: (i, 0))
    vec = pl.BlockSpec((1, 8, d), lambda i: ((i >= n_lat_tiles).astype(jnp.int32), 0, 0))
    return row, vec


def modulate(x, vecs, n_lat):
    t, d = x.shape
    tr = _pick(math.gcd(t, n_lat), (256, 128, 64, 32, 16, 8))
    row, vec = _row_specs(tr, d, n_lat // tr)
    return pl.pallas_call(
        _modulate_kernel, grid=(t // tr,), in_specs=[row, vec], out_specs=row,
        out_shape=jax.ShapeDtypeStruct((t, d), BF16),
        compiler_params=_params("parallel"), name="modulate",
    )(x, vecs)


def _row_copy(src_hbm, row, dst_vmem, dst_row, sem):
    return pltpu.make_async_copy(src_hbm.at[pl.ds(row, 1)], dst_vmem.at[pl.ds(dst_row, 1)], sem)


def _combine_ln_kernel(pos_ref, x_ref, y_hbm, v_ref, xo_ref, *rest, tr, n_tiles):
    h_refs, (ybuf, sems) = rest[:-2], rest[-2:]
    i = pl.program_id(0)
    slot = i % 2

    def issue(tile, to_slot):
        def body(r, carry):
            t = tile * tr + r
            _row_copy(y_hbm, pos_ref[2 * t], ybuf.at[to_slot], r, sems.at[to_slot]).start()
            _row_copy(y_hbm, pos_ref[2 * t + 1], ybuf.at[to_slot], tr + r, sems.at[to_slot]).start()
            return carry
        lax.fori_loop(0, tr, body, 0, unroll=8)

    @pl.when(i == 0)
    def _():
        issue(0, 0)

    @pl.when(i + 1 < n_tiles)
    def _():
        issue(i + 1, 1 - slot)

    pltpu.make_async_copy(y_hbm.at[pl.ds(0, 2 * tr)], ybuf.at[slot], sems.at[slot]).wait()
    y = ybuf[slot, :tr] + ybuf[slot, tr:]
    xn, h = _deepnorm(x_ref[...], y, v_ref[0])
    xo_ref[...] = xn
    if h_refs:
        h_refs[0][...] = h.astype(h_refs[0].dtype)


def combine_ln(x, y_sorted, pair_pos, vecs, n_lat, rows, emit_h=True):
    d = x.shape[1]
    tr = _pick(math.gcd(rows, n_lat), (128, 64, 32, 16, 8))
    n_tiles = rows // tr
    n_lat_tiles = n_lat // tr
    row = pl.BlockSpec((tr, d), lambda i, pos: (i, 0))
    vec = pl.BlockSpec((1, 8, d), lambda i, pos: ((i >= n_lat_tiles).astype(jnp.int32), 0, 0))
    out_shape = [jax.ShapeDtypeStruct((rows, d), F32)]
    out_specs = [row]
    if emit_h:
        out_shape.append(jax.ShapeDtypeStruct((rows, d), BF16))
        out_specs.append(row)
    return pl.pallas_call(
        functools.partial(_combine_ln_kernel, tr=tr, n_tiles=n_tiles),
        grid_spec=pltpu.PrefetchScalarGridSpec(
            num_scalar_prefetch=1, grid=(n_tiles,),
            in_specs=[row, pl.BlockSpec(memory_space=pl.ANY), vec],
            out_specs=out_specs,
            scratch_shapes=[pltpu.VMEM((2, 2 * tr, d), F32), pltpu.SemaphoreType.DMA((2,))]),
        out_shape=out_shape, compiler_params=_params("arbitrary"), name="combine_ln",
    )(pair_pos, x, y_sorted, vecs)


def _mm_kernel(*refs, n_extra, prologue, epilogue, transpose_out):
    a_ref, w_ref = refs[0], refs[1]
    extras = refs[2:2 + n_extra]
    o_ref, wb_ref = refs[2 + n_extra], refs[3 + n_extra]

    @pl.when(pl.program_id(1) == 0)
    def _():
        wb_ref[...] = w_ref[...].astype(BF16)

    a = a_ref[...]
    if prologue is not None:
        a = prologue(a, *extras)
    acc = jnp.dot(a.astype(BF16), wb_ref[...], preferred_element_type=F32)
    if epilogue is not None:
        acc = epilogue(acc, *extras)
    if transpose_out:
        acc = acc.T
    o_ref[...] = acc.astype(o_ref.dtype)


def matmul(a, w, *, rows, col0, n_cols, out_dtype, prologue=None, epilogue=None, extras=(),
           extra_specs=(), out_mult=1, transpose_out=False, name="matmul"):
    k = a.shape[1]
    tm = _pick(rows, (1024, 768, 512, 256, 128))
    tn = _pick(math.gcd(n_cols, col0) if col0 else n_cols, (512, 256, 128))
    cb0 = col0 // tn
    kern = functools.partial(_mm_kernel, n_extra=len(extras), prologue=prologue, epilogue=epilogue,
                             transpose_out=transpose_out)
    if transpose_out:
        out_spec = pl.BlockSpec((tn * out_mult, tm), lambda j, i: (j, i))
        out_shape = jax.ShapeDtypeStruct((n_cols * out_mult, rows), out_dtype)
    else:
        out_spec = pl.BlockSpec((tm, tn * out_mult), lambda j, i: (i, j))
        out_shape = jax.ShapeDtypeStruct((rows, n_cols * out_mult), out_dtype)
    return pl.pallas_call(
        kern,
        grid=(n_cols // tn, rows // tm),
        in_specs=[pl.BlockSpec((tm, k), lambda j, i: (i, 0)),
                  pl.BlockSpec((k, tn), lambda j, i: (0, cb0 + j)),
                  *extra_specs],
        out_specs=out_spec,
        out_shape=out_shape,
        scratch_shapes=[pltpu.VMEM((k, tn), BF16)],
        compiler_params=_params("parallel", "arbitrary"),
        name=name,
    )(a, w, *extras)


def _swap_pairs(x, half):
    n = x.shape[-1]
    lane = lax.broadcasted_iota(jnp.int32, x.shape, x.ndim - 1)
    fwd = pltpu.roll(x, n - half, axis=x.ndim - 1)
    bwd = pltpu.roll(x, half, axis=x.ndim - 1)
    return jnp.where((lane % (2 * half)) < half, fwd, bwd)


def _rope(x, cos, sin_signed, half):
    return x * cos + _swap_pairs(x, half) * sin_signed


def _rms(x, gain):
    return x * lax.rsqrt(jnp.mean(x * x, axis=-1, keepdims=True) + EPS) * gain


def _head_epilogue(acc, cos_ref, sin_ref, gain_ref, *, half, scale, norm, rope_chunks, split_halves):
    outs = []
    for c in range(acc.shape[1] // LANES):
        x = acc[:, c * LANES:(c + 1) * LANES]
        if norm:
            x = _rms(x, gain_ref[...])
        if rope_chunks is None or (c % rope_chunks[1]) == rope_chunks[0]:
            x = _rope(x, cos_ref[...], sin_ref[...], half)
        if scale != 1.0:
            x = x * scale
        if split_halves:
            lane = lax.broadcasted_iota(jnp.int32, x.shape, 1)
            outs.append(jnp.where(lane < LANES // 2, x, 0.0))
            outs.append(jnp.where(lane >= LANES // 2, x, 0.0))
        else:
            outs.append(x)
    return jnp.concatenate(outs, axis=1) if len(outs) > 1 else outs[0]


def project_heads(a, w, tables, gain, *, rows, col0, n_cols, half, scale, norm=False,
                  rope_chunks=None, split_halves=False, name="proj"):
    cos, sin = tables
    tm = _pick(rows, (1024, 768, 512, 256, 128))
    if gain is None:
        gain = jnp.ones((1, LANES), F32)
    epi = functools.partial(_head_epilogue, half=half, scale=scale, norm=norm, rope_chunks=rope_chunks,
                            split_halves=split_halves)
    return matmul(
        a, w, rows=rows, col0=col0, n_cols=n_cols, out_dtype=BF16, epilogue=epi,
        extras=(cos, sin, gain),
        extra_specs=(pl.BlockSpec((tm, LANES), lambda j, i: (i, 0)),
                     pl.BlockSpec((tm, LANES), lambda j, i: (i, 0)),
                     pl.BlockSpec((1, LANES), lambda j, i: (0, 0))),
        out_mult=2 if split_halves else 1, name=name)


def rms_matmul(a, gain, w, *, rows, n_cols, out_dtype, epilogue=None, extras=(), extra_specs=(),
               transpose_out=False, name="rms_mm"):
    k = a.shape[1]
    pro = lambda a_t, g_ref, *rest: _rms(a_t, g_ref[...])
    epi = None if epilogue is None else (lambda acc, g_ref, *rest: epilogue(acc, *rest))
    return matmul(a, w, rows=rows, col0=0, n_cols=n_cols, out_dtype=out_dtype, prologue=pro, epilogue=epi,
                  extras=(gain.reshape(1, k), *extras),
                  extra_specs=(pl.BlockSpec((1, k), lambda j, i: (0, 0)), *extra_specs),
                  transpose_out=transpose_out, name=name)


def _kq(k, q):
    return lax.dot_general(k, q, (((1,), (1,)), ((), ())), preferred_element_type=F32)


def _col_reduce(x, op):
    rows, n = x.shape
    parts = 8 if rows % 64 == 0 else 1
    if parts > 1:
        x = op(x.reshape(parts, rows // parts, n), axis=1)
    return op(x, axis=0, keepdims=True)


def _flash_kernel(*refs, n_kv, G, dq, has_k2, diff, nq_lat, nk, kc, qcols):
    refs = list(refs)
    if diff is not None:
        lam_ref, sub_ref = refs.pop(0), refs.pop(0)
    q_ref, k_ref = refs.pop(0), refs.pop(0)
    k2_ref = refs.pop(0) if has_k2 else None
    vt_ref, kx_ref = refs.pop(0), refs.pop(0)
    kx2_ref = refs.pop(0) if has_k2 else None
    vxt_ref = refs.pop(0)
    o_ref, m_scr, acc_scr = refs
    i, j = pl.program_id(1), pl.program_id(2)
    tq = q_ref.shape[0]

    def folded_q(s):
        parts = [q_ref[:, (s * G + g) * dq:(s * G + g + 1) * dq] for g in range(G)]
        return jnp.concatenate(parts, axis=0) if G > 1 else parts[0]

    def scores(qs, k_r, k2_r, s, rows, cols):
        k = k_r[rows, s * LANES:(s + 1) * LANES]
        if k2_r is not None:
            k = jnp.concatenate([k, k2_r[rows, :]], axis=1)
        return _kq(k, qs[s][cols])

    def update(s, cols, st, vt):
        m_prev = m_scr[s, :, cols]
        m_new = jnp.maximum(m_prev, _col_reduce(st, jnp.max))
        p = jnp.exp(st - m_new).astype(BF16)
        alpha = jnp.exp(m_prev - m_new)
        vt1 = jnp.concatenate([vt, jnp.ones((ONES_ROWS, vt.shape[1]), BF16)], axis=0)
        acc_scr[s, :, cols] = acc_scr[s, :, cols] * alpha + jnp.dot(vt1, p, preferred_element_type=F32)
        m_scr[s, :, cols] = m_new

    def run(k_r, k2_r, vt_r, row_slices):
        qs = [folded_q(s) for s in range(n_kv)]
        col_slices = [slice(c0, c0 + qcols) for c0 in range(0, G * tq, qcols)]
        items = [(s, rows, cols) for rows in row_slices for s in range(n_kv) for cols in col_slices]
        st_next = scores(qs, k_r, k2_r, *items[0])
        for t, (s, rows, cols) in enumerate(items):
            st = st_next
            if t + 1 < len(items):
                st_next = scores(qs, k_r, k2_r, *items[t + 1])
            update(s, cols, st, vt_r[s * LANES:(s + 1) * LANES, rows])

    @pl.when(j == 0)
    def _():
        m_scr[...] = jnp.full(m_scr.shape, NEG_INF, F32)
        acc_scr[...] = jnp.zeros(acc_scr.shape, F32)
        run(kx_ref, kx2_ref, vxt_ref, [slice(None)])

    @pl.when(i < nq_lat)
    def _():
        run(k_ref, k2_ref, vt_ref, [slice(cc * kc, (cc + 1) * kc) for cc in range(k_ref.shape[0] // kc)])

    def normalized(s):
        return acc_scr[s, :LANES] / acc_scr[s, LANES:LANES + 1]

    @pl.when(j == nk - 1)
    def _():
        if diff is None:
            for s in range(n_kv):
                o = normalized(s)
                for g in range(G):
                    c = s * G + g
                    o_ref[:, c * LANES:(c + 1) * LANES] = o[:, g * tq:(g + 1) * tq].T.astype(o_ref.dtype)
        else:
            lv = lam_ref[...]
            lam = (jnp.exp(jnp.sum(lv[0:1] * lv[1:2], axis=1, keepdims=True))
                   - jnp.exp(jnp.sum(lv[2:3] * lv[3:4], axis=1, keepdims=True)) + diff)
            for s in range(n_kv):
                o = normalized(s)
                o = o[:, :tq] - lam * o[:, tq:]
                o = _rms(o.T, sub_ref[...]) * (1.0 - diff)
                o_ref[:, s * LANES:(s + 1) * LANES] = o.astype(o_ref.dtype)


def flash_attention(q, k, vt, *, k2=None, diff=None, n_groups, n_kv, G, dq, n_lat, n_ctx,
                    with_ctx_q, name):
    tq = n_ctx
    nq_lat = n_lat // tq
    nqb = nq_lat + (1 if with_ctx_q else 0)
    tk = _pick(n_lat, (1024, 512, 256, 128))
    nk = n_lat // tk
    kc = min(tk, 512)
    cb = n_lat // n_ctx
    kw = n_kv * LANES
    n_slots = n_kv * G
    ow = (n_kv if diff is not None else n_slots) * LANES

    def jmap(i, j):
        return jnp.where(i < nq_lat, j, 0)

    args, specs = [], []
    if diff is not None:
        lam_vecs, subln, lambda_init = diff
        args += [lam_vecs, subln.reshape(1, LANES)]
        specs += [pl.BlockSpec(lam_vecs.shape, lambda h, i, j: (0, 0)),
                  pl.BlockSpec((1, LANES), lambda h, i, j: (0, 0))]
    args += [q, k]
    specs += [pl.BlockSpec((tq, n_slots * dq), lambda h, i, j: (i, h)),
              pl.BlockSpec((tk, kw), lambda h, i, j: (jmap(i, j), h))]
    if k2 is not None:
        args.append(k2)
        specs.append(pl.BlockSpec((tk, LANES), lambda h, i, j: (jmap(i, j), 0)))
    args += [vt, k]
    specs += [pl.BlockSpec((kw, tk), lambda h, i, j: (h, jmap(i, j))),
              pl.BlockSpec((n_ctx, kw), lambda h, i, j: (cb, h))]
    if k2 is not None:
        args.append(k2)
        specs.append(pl.BlockSpec((n_ctx, LANES), lambda h, i, j: (cb, 0)))
    args.append(vt)
    specs.append(pl.BlockSpec((kw, n_ctx), lambda h, i, j: (h, cb)))
    kern = functools.partial(_flash_kernel, n_kv=n_kv, G=G, dq=dq, has_k2=k2 is not None,
                             diff=None if diff is None else diff[2],
                             nq_lat=nq_lat, nk=nk, kc=kc, qcols=min(G * tq, QUERY_COLS))
    return pl.pallas_call(
        kern,
        grid=(n_groups, nqb, nk),
        in_specs=specs,
        out_specs=pl.BlockSpec((tq, ow), lambda h, i, j: (i, h)),
        out_shape=jax.ShapeDtypeStruct((nqb * tq, n_groups * ow), BF16),
        scratch_shapes=[pltpu.VMEM((n_kv, 1, G * tq), F32),
                        pltpu.VMEM((n_kv, LANES + ONES_ROWS, G * tq), F32)],
        compiler_params=_params("parallel", "parallel", "arbitrary"),
        name=name,
    )(*args)


def _qk(q, k):
    return lax.dot_general(q, k, (((1,), (1,)), ((), ())), preferred_element_type=F32)


def _window_kernel(sink_ref, q_ref, kp_ref, kc_ref, kn_ref, kx_ref, vp_ref, vc_ref, vn_ref, vx_ref,
                   o_ref, *, G, tq, nq):
    h, i = pl.program_id(0), pl.program_id(1)
    k = jnp.concatenate([kp_ref[...], kc_ref[...], kn_ref[...], kx_ref[...]], axis=0)
    v = jnp.concatenate([vp_ref[...], vc_ref[...], vn_ref[...], vx_ref[...]], axis=0)
    n_band = tq + 2 * WINDOW
    r = lax.broadcasted_iota(jnp.int32, (tq, k.shape[0]), 0)
    c = lax.broadcasted_iota(jnp.int32, (tq, k.shape[0]), 1)
    rel = c - WINDOW - r
    kpos = i * tq - WINDOW + c
    band = (jnp.abs(rel) <= WINDOW) & (kpos >= 0) & (kpos < nq * tq) & (i * tq + r < nq * tq)
    mask = band | (c >= n_band)
    for g in range(G):
        s = _qk(q_ref[:, g * HEAD_DIM:(g + 1) * HEAD_DIM], k)
        s = jnp.where(mask, s, NEG_INF)
        sk = sink_ref[h * G + g]
        m = jnp.maximum(jnp.max(s, axis=1, keepdims=True), sk)
        p = jnp.exp(s - m)
        l = jnp.sum(p, axis=1, keepdims=True) + jnp.exp(sk - m)
        o = jnp.dot(p.astype(BF16), v, preferred_element_type=F32) / l
        o_ref[:, g * HEAD_DIM:(g + 1) * HEAD_DIM] = o.astype(o_ref.dtype)


def window_attention(q, k, v, sink, *, n_heads_kv, G, n_lat, n_ctx, with_ctx_q):
    tq = 2 * WINDOW
    assert n_ctx == tq
    nq = n_lat // tq
    nqb = nq + (1 if with_ctx_q else 0)
    nb = n_lat // WINDOW
    cb = n_lat // n_ctx
    w = WINDOW

    def prev_map(h, i):
        return (jnp.maximum(2 * i - 1, 0), h)

    def next_map(h, i):
        return (jnp.minimum(2 * i + 2, nb - 1), h)

    kv_specs = [pl.BlockSpec((w, HEAD_DIM), prev_map),
                pl.BlockSpec((tq, HEAD_DIM), lambda h, i: (i, h)),
                pl.BlockSpec((w, HEAD_DIM), next_map),
                pl.BlockSpec((n_ctx, HEAD_DIM), lambda h, i: (cb, h))]
    kern = functools.partial(_window_kernel, G=G, tq=tq, nq=nq)
    return pl.pallas_call(
        kern,
        grid=(n_heads_kv, nqb),
        in_specs=[pl.BlockSpec(memory_space=pltpu.SMEM),
                  pl.BlockSpec((tq, G * HEAD_DIM), lambda h, i: (i, h)),
                  *kv_specs, *kv_specs],
        out_specs=pl.BlockSpec((tq, G * HEAD_DIM), lambda h, i: (i, h)),
        out_shape=jax.ShapeDtypeStruct((nqb * tq, n_heads_kv * G * HEAD_DIM), BF16),
        compiler_params=_params("parallel", "parallel"),
        name="window_attention",
    )(sink, q, k, k, k, k, v, v, v, v)


def moe_plan(hx, d, tm):
    rows = hx.shape[0]
    n_pairs = 2 * rows
    n_tiles = n_pairs // tm + N_EXPERTS
    e_flat = hx[:, d + N_EXPERTS:d + N_EXPERTS + 2].astype(jnp.int32).reshape(n_pairs)
    order = jnp.argsort(e_flat, stable=True).astype(jnp.int32)
    e_sorted = e_flat[order]
    counts = jnp.sum(e_flat[:, None] == jnp.arange(N_EXPERTS, dtype=jnp.int32)[None, :], axis=0, dtype=jnp.int32)
    tiles_e = (counts + tm - 1) // tm
    tile_end = jnp.cumsum(tiles_e)
    tile_start = tile_end - tiles_e
    seg_start = jnp.cumsum(counts) - counts
    slot_sorted = tile_start[e_sorted] * tm + jnp.arange(n_pairs, dtype=jnp.int32) - seg_start[e_sorted]
    src_tok = jnp.zeros((n_tiles * tm,), jnp.int32).at[slot_sorted].set(order // 2)
    pair_slot = jnp.zeros((n_pairs,), jnp.int32).at[order].set(slot_sorted)
    n_used = tile_end[-1]
    tile_ids = jnp.minimum(jnp.arange(n_tiles, dtype=jnp.int32), n_used - 1)
    tile_expert = jnp.sum(tile_ids[:, None] >= tile_end[None, :], axis=1, dtype=jnp.int32)
    return src_tok, tile_expert, n_used.reshape(1), pair_slot


def _moe_experts_kernel(te_ref, src_ref, n_ref, hx_hbm, wgu_ref, wd_ref, o_ref, xbuf, sems, wgu_bf, wd_bf,
                        *, tm, d):
    i = pl.program_id(0)
    n_used = n_ref[0]
    slot = i % 2

    def issue(tile, to_slot):
        def body(r, carry):
            _row_copy(hx_hbm, src_ref[tile * tm + r], xbuf.at[to_slot], r, sems.at[to_slot]).start()
            return carry
        lax.fori_loop(0, tm, body, 0, unroll=8)

    @pl.when(i == 0)
    def _():
        issue(0, 0)

    @pl.when(i + 1 < n_used)
    def _():
        issue(i + 1, 1 - slot)

    @pl.when(i < n_used)
    def _():
        e = te_ref[i]

        @pl.when((i == 0) | (e != te_ref[jnp.maximum(i - 1, 0)]))
        def _():
            wgu_bf[...] = wgu_ref[0, 0].astype(BF16)
            wd_bf[...] = wd_ref[0, 0].astype(BF16)

        pltpu.make_async_copy(hx_hbm.at[pl.ds(0, tm)], xbuf.at[slot], sems.at[slot]).wait()
        xg = xbuf[slot]
        gu = jnp.dot(xg[:, :d].astype(BF16), wgu_bf[...], preferred_element_type=F32)
        f = gu.shape[1] // 2
        gate, up = gu[:, :f], gu[:, f:]
        comb = xg[:, d:]
        lane = lax.broadcasted_iota(jnp.int32, comb.shape, 1)
        c = jnp.sum(jnp.where(lane == e, comb, 0.0), axis=1, keepdims=True)
        act = (gate * jax.nn.sigmoid(gate)) * up * c
        o_ref[...] = jnp.dot(act.astype(BF16), wd_bf[...], preferred_element_type=F32)

    @pl.when(i >= n_used)
    def _():
        o_ref[...] = jnp.zeros(o_ref.shape, o_ref.dtype)


def moe_experts(hx, plan, w_gate_up, w_down, layer, tm):
    src_tok, tile_expert, n_used, _ = plan
    d = hx.shape[1] - LANES
    f2 = w_gate_up.shape[3]
    n_tiles = tile_expert.shape[0]
    return pl.pallas_call(
        functools.partial(_moe_experts_kernel, tm=tm, d=d),
        grid_spec=pltpu.PrefetchScalarGridSpec(
            num_scalar_prefetch=3, grid=(n_tiles,),
            in_specs=[pl.BlockSpec(memory_space=pl.ANY),
                      pl.BlockSpec((1, 1, d, f2), lambda i, te, src, n: (layer, te[i], 0, 0)),
                      pl.BlockSpec((1, 1, f2 // 2, d), lambda i, te, src, n: (layer, te[i], 0, 0))],
            out_specs=pl.BlockSpec((tm, d), lambda i, te, src, n: (i, 0)),
            scratch_shapes=[pltpu.VMEM((2, tm, d + LANES), F32), pltpu.SemaphoreType.DMA((2,)),
                            pltpu.VMEM((d, f2), BF16), pltpu.VMEM((f2 // 2, d), BF16)]),
        out_shape=jax.ShapeDtypeStruct((n_tiles * tm, d), F32),
        compiler_params=_params("arbitrary"), name="moe_experts",
    )(tile_expert, src_tok, n_used, hx, w_gate_up, w_down)


def _rope_tables(n_lat, n_ctx, rot_dim):
    rows = n_lat // GRID_W
    r, col = jnp.meshgrid(jnp.arange(rows, dtype=F32), jnp.arange(GRID_W, dtype=F32), indexing="ij")
    pos = jnp.stack([r.reshape(-1), col.reshape(-1)], axis=-1)
    n_freq = rot_dim // 4
    inv_freq = ROPE_BASE ** (-jnp.arange(n_freq, dtype=F32) / n_freq)
    ang = pos[:, :, None] * inv_freq
    cos, sin = jnp.cos(ang), jnp.sin(ang)
    cos_full = jnp.concatenate([cos[:, 0], cos[:, 0], cos[:, 1], cos[:, 1]], axis=-1)
    sin_full = jnp.concatenate([-sin[:, 0], sin[:, 0], -sin[:, 1], sin[:, 1]], axis=-1)
    reps = LANES // rot_dim
    cos_full = jnp.tile(cos_full, (1, reps))
    sin_full = jnp.tile(sin_full, (1, reps))
    cos_full = jnp.concatenate([cos_full, jnp.ones((n_ctx, LANES), F32)], axis=0)
    sin_full = jnp.concatenate([sin_full, jnp.zeros((n_ctx, LANES), F32)], axis=0)
    return cos_full, sin_full


def _vec_pack(mods, gate_idx, ln_g, ln_b, next_mods, shift_idx, scale_idx):
    d = mods.shape[-1]
    z = jnp.zeros((2, d), F32)
    gate = mods[:2, gate_idx] if gate_idx is not None else z
    g = jnp.broadcast_to(ln_g, (2, d)) if ln_g is not None else z
    b = jnp.broadcast_to(ln_b, (2, d)) if ln_b is not None else z
    shift = next_mods[:2, shift_idx] if next_mods is not None else z
    scale = next_mods[:2, scale_idx] if next_mods is not None else z
    return jnp.stack([gate, g, b, shift, scale, z, z, z], axis=1)


def kernel(x, c, ctx, c_ctx, ada_w, ada_b, ln_g, ln_b, win_w_qkv, win_w_o, win_sink, qkn_w_qkv, qkn_q_gain, qkn_k_gain, qkn_w_o, mla_w_a, mla_q_gain, mla_kv_gain, mla_w_qb, mla_w_kvb, mla_w_o, diff_w_qkv, diff_lambda, diff_subln, diff_w_o, moe_w_group, moe_b_group, moe_w_expert, moe_b_expert, moe_w_gate_up, moe_w_down):
    b, n_lat, d = x.shape
    n_ctx = ctx.shape[1]
    assert b == 1
    t = n_lat + n_ctx
    n_heads = d // HEAD_DIM
    n_kv = n_heads // 4
    grp = n_heads // n_kv

    tab_h = _rope_tables(n_lat, n_ctx, HEAD_DIM)
    tab_r = _rope_tables(n_lat, n_ctx, C_ROPE)

    cond8 = jnp.zeros((8, d), F32).at[0].set(c[0]).at[1].set(c_ctx)
    mods = modulation_all(cond8, ada_w, ada_b).reshape(DEPTH, 8, N_MOD, d)

    xs = jnp.concatenate([x[0], ctx[0]], axis=0)
    h = modulate(xs, _vec_pack(mods[0], None, None, None, mods[0], 0, 1), n_lat)

    for i in range(DEPTH):
        kind = i % 4
        need_ctx = i < DEPTH - 1
        rows = t if need_ctx else n_lat
        att_kw = dict(n_lat=n_lat, n_ctx=n_ctx, with_ctx_q=need_ctx)
        if kind == 0:
            w = win_w_qkv[0]
            scale = HEAD_DIM ** -0.5
            q = project_heads(h, w, tab_h, None, rows=rows, col0=0, n_cols=d, half=32, scale=scale, name="win_q")
            k = project_heads(h, w, tab_h, None, rows=t, col0=d, n_cols=n_kv * HEAD_DIM, half=32, scale=1.0,
                              name="win_k")
            v = matmul(h, w, rows=t, col0=d + n_kv * HEAD_DIM, n_cols=n_kv * HEAD_DIM, out_dtype=BF16, name="win_v")
            o = window_attention(q, k, v, win_sink[0], n_heads_kv=n_kv, G=grp, **att_kw)
            w_o = win_w_o[0]
        elif kind == 1:
            w = qkn_w_qkv[0]
            scale = HEAD_DIM ** -0.5
            q = project_heads(h, w, tab_h, qkn_q_gain[0].reshape(1, HEAD_DIM), rows=rows, col0=0, n_cols=d, half=32,
                              scale=scale, norm=True, name="qkn_q")
            k = project_heads(h, w, tab_h, qkn_k_gain[0].reshape(1, HEAD_DIM), rows=t, col0=d,
                              n_cols=n_kv * HEAD_DIM, half=32, scale=1.0, norm=True, name="qkn_k")
            vt = matmul(h, w, rows=t, col0=d + n_kv * HEAD_DIM, n_cols=n_kv * HEAD_DIM, out_dtype=BF16,
                        transpose_out=True, name="qkn_vt")
            o = flash_attention(q, k, vt, n_groups=n_kv, n_kv=1, G=grp, dq=HEAD_DIM, name="qkn_attn", **att_kw)
            w_o = qkn_w_o[0]
        elif kind == 2:
            w_a = mla_w_a[0]
            q_rank = mla_q_gain.shape[1]
            kv_rank = mla_kv_gain.shape[1]
            scale = (C_NOPE + C_ROPE) ** -0.5
            cq = matmul(h, w_a, rows=rows, col0=0, n_cols=q_rank, out_dtype=F32, name="mla_cq")
            ckv = matmul(h, w_a, rows=t, col0=q_rank, n_cols=kv_rank, out_dtype=F32, name="mla_ckv")
            w_pe = jnp.pad(w_a[:, q_rank + kv_rank:], ((0, 0), (0, LANES - C_ROPE)))
            k_pe = project_heads(h, w_pe, tab_r, None, rows=t, col0=0, n_cols=LANES, half=16, scale=1.0,
                                 name="mla_kpe")
            w_qb = mla_w_qb[0].reshape(q_rank, n_heads, C_NOPE + C_ROPE)
            w_qb = jnp.pad(w_qb, ((0, 0), (0, 0), (0, 2 * LANES - C_NOPE - C_ROPE))).reshape(q_rank, n_heads * 2 * LANES)
            w_kvb = mla_w_kvb[0].reshape(kv_rank, n_heads, C_NOPE + C_V)
            w_kn = w_kvb[:, :, :C_NOPE].reshape(kv_rank, n_heads * C_NOPE)
            w_v = w_kvb[:, :, C_NOPE:].reshape(kv_rank, n_heads * C_V)
            tm = _pick(rows, (1024, 768, 512, 256, 128))
            q_epi = functools.partial(_head_epilogue, half=16, scale=scale, norm=False, rope_chunks=(1, 2),
                                      split_halves=False)
            q = rms_matmul(cq, mla_q_gain[0], w_qb, rows=rows, n_cols=n_heads * 2 * LANES, out_dtype=BF16,
                           epilogue=lambda acc, cs, sn: q_epi(acc, cs, sn, None),
                           extras=tab_r,
                           extra_specs=(pl.BlockSpec((tm, LANES), lambda j, i: (i, 0)),
                                        pl.BlockSpec((tm, LANES), lambda j, i: (i, 0))), name="mla_q")
            k_nope = rms_matmul(ckv, mla_kv_gain[0], w_kn, rows=t, n_cols=n_heads * C_NOPE, out_dtype=BF16,
                                name="mla_kn")
            vt = rms_matmul(ckv, mla_kv_gain[0], w_v, rows=t, n_cols=n_heads * C_V, out_dtype=BF16,
                            transpose_out=True, name="mla_vt")
            hp = 4
            o = flash_attention(q, k_nope, vt, k2=k_pe, n_groups=n_heads // hp, n_kv=hp, G=1, dq=2 * LANES,
                                name="mla_attn", **att_kw)
            w_o = mla_w_o[0]
        else:
            w = diff_w_qkv[0]
            lambda_init = 0.8 - 0.6 * math.exp(-0.3 * i)
            scale = DF_HEAD ** -0.5
            n_dh = d // (2 * DF_HEAD)
            q = project_heads(h, w, tab_r, None, rows=rows, col0=0, n_cols=d, half=16, scale=scale,
                              split_halves=True, name="diff_q")
            k = project_heads(h, w, tab_r, None, rows=t, col0=d, n_cols=d, half=16, scale=1.0, name="diff_k")
            vt = matmul(h, w, rows=t, col0=2 * d, n_cols=d, out_dtype=BF16, transpose_out=True, name="diff_vt")
            hp = 2
            o = flash_attention(q, k, vt, diff=(diff_lambda[0], diff_subln[0], lambda_init), n_groups=n_dh // hp,
                                n_kv=hp, G=2, dq=LANES, name="diff_attn", **att_kw)
            w_o = diff_w_o[0]

        a = matmul(o, w_o, rows=rows, col0=0, n_cols=d, out_dtype=F32, name="attn_out")
        w_r = jnp.concatenate([moe_w_expert[i], moe_w_group[i],
                               jnp.zeros((d, LANES - N_EXPERTS - N_GROUPS), F32)], axis=1)
        b_r = jnp.concatenate([moe_b_expert[i], moe_b_group[i],
                               jnp.zeros((LANES - N_EXPERTS - N_GROUPS,), F32)]).reshape(1, LANES)
        xs, hx = ln_router(xs, a, _vec_pack(mods[i], 2, ln_g[i, 0], ln_b[i, 0], mods[i], 3, 4), w_r, b_r,
                           n_lat, rows)
        plan = moe_plan(hx, d, MOE_TILE)
        y_slots = moe_experts(hx, plan, moe_w_gate_up, moe_w_down, i, MOE_TILE)
        if i + 1 < DEPTH:
            xs, h = combine_ln(xs, y_slots, plan[3], _vec_pack(mods[i], 5, ln_g[i, 1], ln_b[i, 1], mods[i + 1], 0, 1),
                               n_lat, rows)
        else:
            (xs,) = combine_ln(xs, y_slots, plan[3], _vec_pack(mods[i], 5, ln_g[i, 1], ln_b[i, 1], None, 0, 1),
                               n_lat, rows, emit_h=False)
    return xs[:n_lat].reshape(b, n_lat, d)
```

```python
import functools
import math

import jax
import jax.numpy as jnp
from jax import lax
from jax.experimental import pallas as pl
from jax.experimental.pallas import tpu as pltpu

F32 = jnp.float32
BF16 = jnp.bfloat16

DEPTH = 4
GRID_W = 64
HEAD_DIM = 128
ROPE_BASE = 10000.0
EPS = 1e-6
NEG_INF = -1e30
N_MOD = 6
WINDOW = 128
C_NOPE = 128
C_ROPE = 64
C_V = 128
DF_HEAD = 64
N_GROUPS = 4
EXPERTS_PER_GROUP = 6
N_EXPERTS = N_GROUPS * EXPERTS_PER_GROUP
ALPHA = (2.0 * DEPTH) ** 0.25
LOG2E = math.log2(math.e)
LANES = 128
MOE_TILE = 256
QUERY_COLS = 1024
ONES_ROWS = 16
VMEM_LIMIT_BYTES = 56 * 1024 * 1024


def _pick(n, cands):
    for c in cands:
        if n % c == 0:
            return c
    raise ValueError(f"no tile in {cands} divides {n}")


def _params(*sem):
    return pltpu.CompilerParams(dimension_semantics=sem, vmem_limit_bytes=VMEM_LIMIT_BYTES)


def _mod_kernel(cond_ref, w_ref, b_ref, o_ref):
    a = cond_ref[...]
    a = (a * jax.nn.sigmoid(a)).astype(BF16)
    o_ref[0] = jnp.dot(a, w_ref[0].astype(BF16), preferred_element_type=F32) + b_ref[0]


def modulation_all(cond8, ada_w, ada_b):
    depth, d, n = ada_w.shape
    tn = _pick(n, (512, 256, 128))
    return pl.pallas_call(
        _mod_kernel,
        grid=(depth, n // tn),
        in_specs=[pl.BlockSpec((8, d), lambda l, j: (0, 0)),
                  pl.BlockSpec((1, d, tn), lambda l, j: (l, 0, j)),
                  pl.BlockSpec((1, 1, tn), lambda l, j: (l, 0, j))],
        out_specs=pl.BlockSpec((1, 8, tn), lambda l, j: (l, 0, j)),
        out_shape=jax.ShapeDtypeStruct((depth, 8, n), F32),
        compiler_params=_params("parallel", "parallel"),
        name="modulation",
    )(cond8, ada_w, ada_b.reshape(depth, 1, n))


def _modulate_kernel(x_ref, v_ref, h_ref):
    v = v_ref[0]
    h_ref[...] = (x_ref[...] * (1.0 + v[4:5]) + v[3:4]).astype(h_ref.dtype)


def _deepnorm(x, y, v):
    z = ALPHA * x + v[0:1] * y
    mu = jnp.mean(z, axis=-1, keepdims=True)
    zc = z - mu
    var = jnp.mean(zc * zc, axis=-1, keepdims=True)
    xn = zc * lax.rsqrt(var + EPS) * v[1:2] + v[2:3]
    return xn, xn * (1.0 + v[4:5]) + v[3:4]


def _route(logits):
    lane = lax.broadcasted_iota(jnp.int32, logits.shape, 1)
    big = jnp.int32(1 << 20)
    is_group = (lane >= N_EXPERTS) & (lane < N_EXPERTS + N_GROUPS)
    gl = jnp.where(is_group, logits, NEG_INF)
    gmax = jnp.max(gl, axis=1, keepdims=True)
    g_idx = jnp.min(jnp.where(gl == gmax, lane, big), axis=1, keepdims=True) - N_EXPERTS
    g_w = 1.0 / jnp.sum(jnp.exp(gl - gmax), axis=1, keepdims=True)
    lo = g_idx * EXPERTS_PER_GROUP
    el = jnp.where((lane >= lo) & (lane < lo + EXPERTS_PER_GROUP), logits, NEG_INF)
    v1 = jnp.max(el, axis=1, keepdims=True)
    i1 = jnp.min(jnp.where(el == v1, lane, big), axis=1, keepdims=True)
    el2 = jnp.where(lane == i1, NEG_INF, el)
    v2 = jnp.max(el2, axis=1, keepdims=True)
    i2 = jnp.min(jnp.where(el2 == v2, lane, big), axis=1, keepdims=True)
    e2 = jnp.exp(v2 - v1)
    den = 1.0 + e2
    w1 = (1.0 / den) * g_w
    w2 = (e2 / den) * g_w
    comb = jnp.where(lane == i1, w1, 0.0) + jnp.where(lane == i2, w2, 0.0)
    comb = jnp.where(lane == N_EXPERTS, i1.astype(F32), comb)
    return jnp.where(lane == N_EXPERTS + 1, i2.astype(F32), comb)


def _ln_router_kernel(x_ref, y_ref, v_ref, wr_ref, br_ref, xo_ref, hx_ref):
    d = x_ref.shape[1]
    xn, h = _deepnorm(x_ref[...], y_ref[...], v_ref[0])
    xo_ref[...] = xn
    logits = jnp.dot(h.astype(BF16), wr_ref[...].astype(BF16), preferred_element_type=F32) + br_ref[...]
    hx_ref[:, :d] = h
    hx_ref[:, d:] = _route(logits)


def ln_router(x, y, vecs, w_r, b_r, n_lat, rows):
    d = x.shape[1]
    tr = _pick(math.gcd(rows, n_lat), (256, 128, 64, 32, 16, 8))
    row, vec = _row_specs(tr, d, n_lat // tr)
    return pl.pallas_call(
        _ln_router_kernel, grid=(rows // tr,),
        in_specs=[row, row, vec, pl.BlockSpec((d, LANES), lambda i: (0, 0)),
                  pl.BlockSpec((1, LANES), lambda i: (0, 0))],
        out_specs=[row, pl.BlockSpec((tr, d + LANES), lambda i: (i, 0))],
        out_shape=[jax.ShapeDtypeStruct((rows, d), F32), jax.ShapeDtypeStruct((rows, d + LANES), F32)],
        compiler_params=_params("parallel"), name="ln_router",
    )(x, y, vecs, w_r, b_r)


def _row_specs(tr, d, n_lat_tiles):
    row = pl.BlockSpec((tr, d), lambda i: (i, 0))
    vec = pl.BlockSpec((1, 8, d), lambda i: ((i >= n_lat_tiles).astype(jnp.int32), 0, 0))
    return row, vec


def modulate(x, vecs, n_lat):
    t, d = x.shape
    tr = _pick(math.gcd(t, n_lat), (256, 128, 64, 32, 16, 8))
    row, vec = _row_specs(tr, d, n_lat // tr)
    return pl.pallas_call(
        _modulate_kernel, grid=(t // tr,), in_specs=[row, vec], out_specs=row,
        out_shape=jax.ShapeDtypeStruct((t, d), BF16),
        compiler_params=_params("parallel"), name="modulate",
    )(x, vecs)


def _row_copy(src_hbm, row, dst_vmem, dst_row, sem):
    return pltpu.make_async_copy(src_hbm.at[pl.ds(row, 1)], dst_vmem.at[pl.ds(dst_row, 1)], sem)


def _combine_ln_kernel(pos_ref, x_ref, y_hbm, v_ref, xo_ref, *rest, tr, n_tiles):
    h_refs, (ybuf, sems) = rest[:-2], rest[-2:]
    i = pl.program_id(0)
    slot = i % 2

    def issue(tile, to_slot):
        def body(r, carry):
            t = tile * tr + r
            _row_copy(y_hbm, pos_ref[2 * t], ybuf.at[to_slot], r, sems.at[to_slot]).start()
            _row_copy(y_hbm, pos_ref[2 * t + 1], ybuf.at[to_slot], tr + r, sems.at[to_slot]).start()
            return carry
        lax.fori_loop(0, tr, body, 0, unroll=8)

    @pl.when(i == 0)
    def _():
        issue(0, 0)

    @pl.when(i + 1 < n_tiles)
    def _():
        issue(i + 1, 1 - slot)

    pltpu.make_async_copy(y_hbm.at[pl.ds(0, 2 * tr)], ybuf.at[slot], sems.at[slot]).wait()
    y = ybuf[slot, :tr] + ybuf[slot, tr:]
    xn, h = _deepnorm(x_ref[...], y, v_ref[0])
    xo_ref[...] = xn
    if h_refs:
        h_refs[0][...] = h.astype(h_refs[0].dtype)


def combine_ln(x, y_sorted, pair_pos, vecs, n_lat, rows, emit_h=True):
    d = x.shape[1]
    tr = _pick(math.gcd(rows, n_lat), (128, 64, 32, 16, 8))
    n_tiles = rows // tr
    n_lat_tiles = n_lat // tr
    row = pl.BlockSpec((tr, d), lambda i, pos: (i, 0))
    vec = pl.BlockSpec((1, 8, d), lambda i, pos: ((i >= n_lat_tiles).astype(jnp.int32), 0, 0))
    out_shape = [jax.ShapeDtypeStruct((rows, d), F32)]
    out_specs = [row]
    if emit_h:
        out_shape.append(jax.ShapeDtypeStruct((rows, d), BF16))
        out_specs.append(row)
    return pl.pallas_call(
        functools.partial(_combine_ln_kernel, tr=tr, n_tiles=n_tiles),
        grid_spec=pltpu.PrefetchScalarGridSpec(
            num_scalar_prefetch=1, grid=(n_tiles,),
            in_specs=[row, pl.BlockSpec(memory_space=pl.ANY), vec],
            out_specs=out_specs,
            scratch_shapes=[pltpu.VMEM((2, 2 * tr, d), F32), pltpu.SemaphoreType.DMA((2,))]),
        out_shape=out_shape, compiler_params=_params("arbitrary"), name="combine_ln",
    )(pair_pos, x, y_sorted, vecs)


def _mm_kernel(*refs, n_extra, prologue, epilogue, transpose_out):
    a_ref, w_ref = refs[0], refs[1]
    extras = refs[2:2 + n_extra]
    o_ref, wb_ref = refs[2 + n_extra], refs[3 + n_extra]

    @pl.when(pl.program_id(1) == 0)
    def _():
        wb_ref[...] = w_ref[...].astype(BF16)

    a = a_ref[...]
    if prologue is not None:
        a = prologue(a, *extras)
    acc = jnp.dot(a.astype(BF16), wb_ref[...], preferred_element_type=F32)
    if epilogue is not None:
        acc = epilogue(acc, *extras)
    if transpose_out:
        acc = acc.T
    o_ref[...] = acc.astype(o_ref.dtype)


def matmul(a, w, *, rows, col0, n_cols, out_dtype, prologue=None, epilogue=None, extras=(),
           extra_specs=(), out_mult=1, transpose_out=False, name="matmul"):
    k = a.shape[1]
    tm = _pick(rows, (1024, 768, 512, 256, 128))
    tn = _pick(math.gcd(n_cols, col0) if col0 else n_cols, (512, 256, 128))
    cb0 = col0 // tn
    kern = functools.partial(_mm_kernel, n_extra=len(extras), prologue=prologue, epilogue=epilogue,
                             transpose_out=transpose_out)
    if transpose_out:
        out_spec = pl.BlockSpec((tn * out_mult, tm), lambda j, i: (j, i))
        out_shape = jax.ShapeDtypeStruct((n_cols * out_mult, rows), out_dtype)
    else:
        out_spec = pl.BlockSpec((tm, tn * out_mult), lambda j, i: (i, j))
        out_shape = jax.ShapeDtypeStruct((rows, n_cols * out_mult), out_dtype)
    return pl.pallas_call(
        kern,
        grid=(n_cols // tn, rows // tm),
        in_specs=[pl.BlockSpec((tm, k), lambda j, i: (i, 0)),
                  pl.BlockSpec((k, tn), lambda j, i: (0, cb0 + j)),
                  *extra_specs],
        out_specs=out_spec,
        out_shape=out_shape,
        scratch_shapes=[pltpu.VMEM((k, tn), BF16)],
        compiler_params=_params("parallel", "arbitrary"),
        name=name,
    )(a, w, *extras)


def _swap_pairs(x, half):
    n = x.shape[-1]
    lane = lax.broadcasted_iota(jnp.int32, x.shape, x.ndim - 1)
    fwd = pltpu.roll(x, n - half, axis=x.ndim - 1)
    bwd = pltpu.roll(x, half, axis=x.ndim - 1)
    return jnp.where((lane % (2 * half)) < half, fwd, bwd)


def _rope(x, cos, sin_signed, half):
    return x * cos + _swap_pairs(x, half) * sin_signed


def _rms(x, gain):
    return x * lax.rsqrt(jnp.mean(x * x, axis=-1, keepdims=True) + EPS) * gain


def _head_epilogue(acc, cos_ref, sin_ref, gain_ref, *, half, scale, norm, rope_chunks, split_halves):
    outs = []
    for c in range(acc.shape[1] // LANES):
        x = acc[:, c * LANES:(c + 1) * LANES]
        if norm:
            x = _rms(x, gain_ref[...])
        if rope_chunks is None or (c % rope_chunks[1]) == rope_chunks[0]:
            x = _rope(x, cos_ref[...], sin_ref[...], half)
        if scale != 1.0:
            x = x * scale
        if split_halves:
            lane = lax.broadcasted_iota(jnp.int32, x.shape, 1)
            outs.append(jnp.where(lane < LANES // 2, x, 0.0))
            outs.append(jnp.where(lane >= LANES // 2, x, 0.0))
        else:
            outs.append(x)
    return jnp.concatenate(outs, axis=1) if len(outs) > 1 else outs[0]


def project_heads(a, w, tables, gain, *, rows, col0, n_cols, half, scale, norm=False,
                  rope_chunks=None, split_halves=False, name="proj"):
    cos, sin = tables
    tm = _pick(rows, (1024, 768, 512, 256, 128))
    if gain is None:
        gain = jnp.ones((1, LANES), F32)
    epi = functools.partial(_head_epilogue, half=half, scale=scale, norm=norm, rope_chunks=rope_chunks,
                            split_halves=split_halves)
    return matmul(
        a, w, rows=rows, col0=col0, n_cols=n_cols, out_dtype=BF16, epilogue=epi,
        extras=(cos, sin, gain),
        extra_specs=(pl.BlockSpec((tm, LANES), lambda j, i: (i, 0)),
                     pl.BlockSpec((tm, LANES), lambda j, i: (i, 0)),
                     pl.BlockSpec((1, LANES), lambda j, i: (0, 0))),
        out_mult=2 if split_halves else 1, name=name)


def rms_matmul(a, gain, w, *, rows, n_cols, out_dtype, epilogue=None, extras=(), extra_specs=(),
               transpose_out=False, name="rms_mm"):
    k = a.shape[1]
    pro = lambda a_t, g_ref, *rest: _rms(a_t, g_ref[...])
    epi = None if epilogue is None else (lambda acc, g_ref, *rest: epilogue(acc, *rest))
    return matmul(a, w, rows=rows, col0=0, n_cols=n_cols, out_dtype=out_dtype, prologue=pro, epilogue=epi,
                  extras=(gain.reshape(1, k), *extras),
                  extra_specs=(pl.BlockSpec((1, k), lambda j, i: (0, 0)), *extra_specs),
                  transpose_out=transpose_out, name=name)


def _kq(k, q):
    return lax.dot_general(k, q, (((1,), (1,)), ((), ())), preferred_element_type=F32)


def _col_reduce(x, op):
    rows, n = x.shape
    parts = 8 if rows % 64 == 0 else 1
    if parts > 1:
        x = op(x.reshape(parts, rows // parts, n), axis=1)
    return op(x, axis=0, keepdims=True)


def _flash_kernel(*refs, n_kv, G, dq, has_k2, diff, nq_lat, nk, kc, qcols):
    refs = list(refs)
    if diff is not None:
        lam_ref, sub_ref = refs.pop(0), refs.pop(0)
    q_ref, k_ref = refs.pop(0), refs.pop(0)
    k2_ref = refs.pop(0) if has_k2 else None
    vt_ref, kx_ref = refs.pop(0), refs.pop(0)
    kx2_ref = refs.pop(0) if has_k2 else None
    vxt_ref = refs.pop(0)
    o_ref, m_scr, acc_scr = refs
    i, j = pl.program_id(1), pl.program_id(2)
    tq = q_ref.shape[0]

    def folded_q(s):
        parts = [q_ref[:, (s * G + g) * dq:(s * G + g + 1) * dq] for g in range(G)]
        return jnp.concatenate(parts, axis=0) if G > 1 else parts[0]

    def scores(qs, k_r, k2_r, s, rows, cols):
        k = k_r[rows, s * LANES:(s + 1) * LANES]
        if k2_r is not None:
            k = jnp.concatenate([k, k2_r[rows, :]], axis=1)
        return _kq(k, qs[s][cols])

    def update(s, cols, st, vt):
        m_prev = m_scr[s, :, cols]
        m_new = jnp.maximum(m_prev, _col_reduce(st, jnp.max))
        p = jnp.exp2((st - m_new).astype(BF16))
        alpha = jnp.exp2(m_prev - m_new)
        vt1 = jnp.concatenate([vt, jnp.ones((ONES_ROWS, vt.shape[1]), BF16)], axis=0)
        acc_scr[s, :, cols] = acc_scr[s, :, cols] * alpha + jnp.dot(vt1, p, preferred_element_type=F32)
        m_scr[s, :, cols] = m_new

    def run(k_r, k2_r, vt_r, row_slices):
        qs = [folded_q(s) for s in range(n_kv)]
        col_slices = [slice(c0, c0 + qcols) for c0 in range(0, G * tq, qcols)]
        items = [(s, rows, cols) for rows in row_slices for s in range(n_kv) for cols in col_slices]
        st_next = scores(qs, k_r, k2_r, *items[0])
        for t, (s, rows, cols) in enumerate(items):
            st = st_next
            if t + 1 < len(items):
                st_next = scores(qs, k_r, k2_r, *items[t + 1])
            update(s, cols, st, vt_r[s * LANES:(s + 1) * LANES, rows])

    @pl.when(j == 0)
    def _():
        m_scr[...] = jnp.full(m_scr.shape, NEG_INF, F32)
        acc_scr[...] = jnp.zeros(acc_scr.shape, F32)
        run(kx_ref, kx2_ref, vxt_ref, [slice(None)])

    @pl.when(i < nq_lat)
    def _():
        run(k_ref, k2_ref, vt_ref, [slice(cc * kc, (cc + 1) * kc) for cc in range(k_ref.shape[0] // kc)])

    def normalized(s):
        return acc_scr[s, :LANES] / acc_scr[s, LANES:LANES + 1]

    @pl.when(j == nk - 1)
    def _():
        if diff is None:
            for s in range(n_kv):
                o = normalized(s)
                for g in range(G):
                    c = s * G + g
                    o_ref[:, c * LANES:(c + 1) * LANES] = o[:, g * tq:(g + 1) * tq].T.astype(o_ref.dtype)
        else:
            lv = lam_ref[...]
            lam = (jnp.exp(jnp.sum(lv[0:1] * lv[1:2], axis=1, keepdims=True))
                   - jnp.exp(jnp.sum(lv[2:3] * lv[3:4], axis=1, keepdims=True)) + diff)
            for s in range(n_kv):
                o = normalized(s)
                o = o[:, :tq] - lam * o[:, tq:]
                o = _rms(o.T, sub_ref[...]) * (1.0 - diff)
                o_ref[:, s * LANES:(s + 1) * LANES] = o.astype(o_ref.dtype)


def flash_attention(q, k, vt, *, k2=None, diff=None, n_groups, n_kv, G, dq, n_lat, n_ctx,
                    with_ctx_q, name):
    tq = n_ctx
    nq_lat = n_lat // tq
    nqb = nq_lat + (1 if with_ctx_q else 0)
    tk = _pick(n_lat, (2048, 1024, 512, 256, 128))
    nk = n_lat // tk
    kc = min(tk, 512)
    cb = n_lat // n_ctx
    kw = n_kv * LANES
    n_slots = n_kv * G
    ow = (n_kv if diff is not None else n_slots) * LANES

    def jmap(i, j):
        return jnp.where(i < nq_lat, j, 0)

    args, specs = [], []
    if diff is not None:
        lam_vecs, subln, lambda_init = diff
        args += [lam_vecs, subln.reshape(1, LANES)]
        specs += [pl.BlockSpec(lam_vecs.shape, lambda h, i, j: (0, 0)),
                  pl.BlockSpec((1, LANES), lambda h, i, j: (0, 0))]
    args += [q, k]
    specs += [pl.BlockSpec((tq, n_slots * dq), lambda h, i, j: (i, h)),
              pl.BlockSpec((tk, kw), lambda h, i, j: (jmap(i, j), h))]
    if k2 is not None:
        args.append(k2)
        specs.append(pl.BlockSpec((tk, LANES), lambda h, i, j: (jmap(i, j), 0)))
    args += [vt, k]
    specs += [pl.BlockSpec((kw, tk), lambda h, i, j: (h, jmap(i, j))),
              pl.BlockSpec((n_ctx, kw), lambda h, i, j: (cb, h))]
    if k2 is not None:
        args.append(k2)
        specs.append(pl.BlockSpec((n_ctx, LANES), lambda h, i, j: (cb, 0)))
    args.append(vt)
    specs.append(pl.BlockSpec((kw, n_ctx), lambda h, i, j: (h, cb)))
    kern = functools.partial(_flash_kernel, n_kv=n_kv, G=G, dq=dq, has_k2=k2 is not None,
                             diff=None if diff is None else diff[2],
                             nq_lat=nq_lat, nk=nk, kc=kc, qcols=min(G * tq, QUERY_COLS))
    return pl.pallas_call(
        kern,
        grid=(n_groups, nqb, nk),
        in_specs=specs,
        out_specs=pl.BlockSpec((tq, ow), lambda h, i, j: (i, h)),
        out_shape=jax.ShapeDtypeStruct((nqb * tq, n_groups * ow), BF16),
        scratch_shapes=[pltpu.VMEM((n_kv, 1, G * tq), F32),
                        pltpu.VMEM((n_kv, LANES + ONES_ROWS, G * tq), F32)],
        compiler_params=_params("parallel", "parallel", "arbitrary"),
        name=name,
    )(*args)


def _qk(q, k):
    return lax.dot_general(q, k, (((1,), (1,)), ((), ())), preferred_element_type=F32)


def _window_kernel(sink_ref, q_ref, kp_ref, kc_ref, kn_ref, kx_ref, vp_ref, vc_ref, vn_ref, vx_ref,
                   o_ref, *, G, tq, nq):
    h, i = pl.program_id(0), pl.program_id(1)
    k = jnp.concatenate([kp_ref[...], kc_ref[...], kn_ref[...], kx_ref[...]], axis=0)
    v = jnp.concatenate([vp_ref[...], vc_ref[...], vn_ref[...], vx_ref[...]], axis=0)
    n_band = tq + 2 * WINDOW
    r = lax.broadcasted_iota(jnp.int32, (tq, k.shape[0]), 0)
    c = lax.broadcasted_iota(jnp.int32, (tq, k.shape[0]), 1)
    rel = c - WINDOW - r
    kpos = i * tq - WINDOW + c
    band = (jnp.abs(rel) <= WINDOW) & (kpos >= 0) & (kpos < nq * tq) & (i * tq + r < nq * tq)
    mask = band | (c >= n_band)
    for g in range(G):
        s = _qk(q_ref[:, g * HEAD_DIM:(g + 1) * HEAD_DIM], k)
        s = jnp.where(mask, s, NEG_INF)
        sk = sink_ref[h * G + g]
        m = jnp.maximum(jnp.max(s, axis=1, keepdims=True), sk)
        p = jnp.exp(s - m)
        l = jnp.sum(p, axis=1, keepdims=True) + jnp.exp(sk - m)
        o = jnp.dot(p.astype(BF16), v, preferred_element_type=F32) / l
        o_ref[:, g * HEAD_DIM:(g + 1) * HEAD_DIM] = o.astype(o_ref.dtype)


def window_attention(q, k, v, sink, *, n_heads_kv, G, n_lat, n_ctx, with_ctx_q):
    tq = 2 * WINDOW
    assert n_ctx == tq
    nq = n_lat // tq
    nqb = nq + (1 if with_ctx_q else 0)
    nb = n_lat // WINDOW
    cb = n_lat // n_ctx
    w = WINDOW

    def prev_map(h, i):
        return (jnp.maximum(2 * i - 1, 0), h)

    def next_map(h, i):
        return (jnp.minimum(2 * i + 2, nb - 1), h)

    kv_specs = [pl.BlockSpec((w, HEAD_DIM), prev_map),
                pl.BlockSpec((tq, HEAD_DIM), lambda h, i: (i, h)),
                pl.BlockSpec((w, HEAD_DIM), next_map),
                pl.BlockSpec((n_ctx, HEAD_DIM), lambda h, i: (cb, h))]
    kern = functools.partial(_window_kernel, G=G, tq=tq, nq=nq)
    return pl.pallas_call(
        kern,
        grid=(n_heads_kv, nqb),
        in_specs=[pl.BlockSpec(memory_space=pltpu.SMEM),
                  pl.BlockSpec((tq, G * HEAD_DIM), lambda h, i: (i, h)),
                  *kv_specs, *kv_specs],
        out_specs=pl.BlockSpec((tq, G * HEAD_DIM), lambda h, i: (i, h)),
        out_shape=jax.ShapeDtypeStruct((nqb * tq, n_heads_kv * G * HEAD_DIM), BF16),
        compiler_params=_params("parallel", "parallel"),
        name="window_attention",
    )(sink, q, k, k, k, k, v, v, v, v)


def moe_plan(hx, d, tm):
    rows = hx.shape[0]
    n_pairs = 2 * rows
    n_tiles = n_pairs // tm + N_EXPERTS
    e_flat = hx[:, d + N_EXPERTS:d + N_EXPERTS + 2].astype(jnp.int32).reshape(n_pairs)
    order = jnp.argsort(e_flat, stable=True).astype(jnp.int32)
    rank = jnp.argsort(order).astype(jnp.int32)
    counts = jnp.sum(e_flat[:, None] == jnp.arange(N_EXPERTS, dtype=jnp.int32)[None, :], axis=0, dtype=jnp.int32)
    tiles_e = (counts + tm - 1) // tm
    tile_end = jnp.cumsum(tiles_e)
    tile_start = tile_end - tiles_e
    seg_start = jnp.cumsum(counts) - counts
    pair_slot = tile_start[e_flat] * tm + rank - seg_start[e_flat]
    n_used = tile_end[-1]
    tile_ids = jnp.minimum(jnp.arange(n_tiles, dtype=jnp.int32), n_used - 1)
    tile_expert = jnp.sum(tile_ids[:, None] >= tile_end[None, :], axis=1, dtype=jnp.int32)
    e_slot = jnp.repeat(tile_expert, tm)
    r_slot = jnp.arange(n_tiles * tm, dtype=jnp.int32) - tile_start[e_slot] * tm
    valid = (r_slot < counts[e_slot]) & (jnp.arange(n_tiles * tm, dtype=jnp.int32) < n_used * tm)
    src_pair = order[jnp.clip(seg_start[e_slot] + r_slot, 0, n_pairs - 1)]
    src_tok = jnp.where(valid, src_pair // 2, 0)
    return src_tok, tile_expert, n_used.reshape(1), pair_slot


def _moe_experts_kernel(te_ref, src_ref, n_ref, hx_hbm, wgu_ref, wd_ref, o_ref, xbuf, sems, wgu_bf, wd_bf,
                        *, tm, d):
    i = pl.program_id(0)
    n_used = n_ref[0]
    slot = i % 2

    def issue(tile, to_slot):
        def body(r, carry):
            _row_copy(hx_hbm, src_ref[tile * tm + r], xbuf.at[to_slot], r, sems.at[to_slot]).start()
            return carry
        lax.fori_loop(0, tm, body, 0, unroll=8)

    @pl.when(i == 0)
    def _():
        issue(0, 0)

    @pl.when(i + 1 < n_used)
    def _():
        issue(i + 1, 1 - slot)

    @pl.when(i < n_used)
    def _():
        e = te_ref[i]

        @pl.when((i == 0) | (e != te_ref[jnp.maximum(i - 1, 0)]))
        def _():
            wgu_bf[...] = wgu_ref[0, 0].astype(BF16)
            wd_bf[...] = wd_ref[0, 0].astype(BF16)

        pltpu.make_async_copy(hx_hbm.at[pl.ds(0, tm)], xbuf.at[slot], sems.at[slot]).wait()
        xg = xbuf[slot]
        gu = jnp.dot(xg[:, :d].astype(BF16), wgu_bf[...], preferred_element_type=F32)
        f = gu.shape[1] // 2
        gate, up = gu[:, :f], gu[:, f:]
        comb = xg[:, d:]
        lane = lax.broadcasted_iota(jnp.int32, comb.shape, 1)
        c = jnp.sum(jnp.where(lane == e, comb, 0.0), axis=1, keepdims=True)
        act = (gate * jax.nn.sigmoid(gate)) * up * c
        o_ref[...] = jnp.dot(act.astype(BF16), wd_bf[...], preferred_element_type=F32)

    @pl.when(i >= n_used)
    def _():
        o_ref[...] = jnp.zeros(o_ref.shape, o_ref.dtype)


def moe_experts(hx, plan, w_gate_up, w_down, layer, tm):
    src_tok, tile_expert, n_used, _ = plan
    d = hx.shape[1] - LANES
    f2 = w_gate_up.shape[3]
    n_tiles = tile_expert.shape[0]
    return pl.pallas_call(
        functools.partial(_moe_experts_kernel, tm=tm, d=d),
        grid_spec=pltpu.PrefetchScalarGridSpec(
            num_scalar_prefetch=3, grid=(n_tiles,),
            in_specs=[pl.BlockSpec(memory_space=pl.ANY),
                      pl.BlockSpec((1, 1, d, f2), lambda i, te, src, n: (layer, te[i], 0, 0)),
                      pl.BlockSpec((1, 1, f2 // 2, d), lambda i, te, src, n: (layer, te[i], 0, 0))],
            out_specs=pl.BlockSpec((tm, d), lambda i, te, src, n: (i, 0)),
            scratch_shapes=[pltpu.VMEM((2, tm, d + LANES), F32), pltpu.SemaphoreType.DMA((2,)),
                            pltpu.VMEM((d, f2), BF16), pltpu.VMEM((f2 // 2, d), BF16)]),
        out_shape=jax.ShapeDtypeStruct((n_tiles * tm, d), F32),
        compiler_params=_params("arbitrary"), name="moe_experts",
    )(tile_expert, src_tok, n_used, hx, w_gate_up, w_down)


def _rope_tables(n_lat, n_ctx, rot_dim):
    rows = n_lat // GRID_W
    r, col = jnp.meshgrid(jnp.arange(rows, dtype=F32), jnp.arange(GRID_W, dtype=F32), indexing="ij")
    pos = jnp.stack([r.reshape(-1), col.reshape(-1)], axis=-1)
    n_freq = rot_dim // 4
    inv_freq = ROPE_BASE ** (-jnp.arange(n_freq, dtype=F32) / n_freq)
    ang = pos[:, :, None] * inv_freq
    cos, sin = jnp.cos(ang), jnp.sin(ang)
    cos_full = jnp.concatenate([cos[:, 0], cos[:, 0], cos[:, 1], cos[:, 1]], axis=-1)
    sin_full = jnp.concatenate([-sin[:, 0], sin[:, 0], -sin[:, 1], sin[:, 1]], axis=-1)
    reps = LANES // rot_dim
    cos_full = jnp.tile(cos_full, (1, reps))
    sin_full = jnp.tile(sin_full, (1, reps))
    cos_full = jnp.concatenate([cos_full, jnp.ones((n_ctx, LANES), F32)], axis=0)
    sin_full = jnp.concatenate([sin_full, jnp.zeros((n_ctx, LANES), F32)], axis=0)
    return cos_full, sin_full


def _vec_pack(mods, gate_idx, ln_g, ln_b, next_mods, shift_idx, scale_idx):
    d = mods.shape[-1]
    z = jnp.zeros((2, d), F32)
    gate = mods[:2, gate_idx] if gate_idx is not None else z
    g = jnp.broadcast_to(ln_g, (2, d)) if ln_g is not None else z
    b = jnp.broadcast_to(ln_b, (2, d)) if ln_b is not None else z
    shift = next_mods[:2, shift_idx] if next_mods is not None else z
    scale = next_mods[:2, scale_idx] if next_mods is not None else z
    return jnp.stack([gate, g, b, shift, scale, z, z, z], axis=1)


def kernel(x, c, ctx, c_ctx, ada_w, ada_b, ln_g, ln_b, win_w_qkv, win_w_o, win_sink, qkn_w_qkv, qkn_q_gain, qkn_k_gain, qkn_w_o, mla_w_a, mla_q_gain, mla_kv_gain, mla_w_qb, mla_w_kvb, mla_w_o, diff_w_qkv, diff_lambda, diff_subln, diff_w_o, moe_w_group, moe_b_group, moe_w_expert, moe_b_expert, moe_w_gate_up, moe_w_down):
    b, n_lat, d = x.shape
    n_ctx = ctx.shape[1]
    assert b == 1
    t = n_lat + n_ctx
    n_heads = d // HEAD_DIM
    n_kv = n_heads // 4
    grp = n_heads // n_kv

    tab_h = _rope_tables(n_lat, n_ctx, HEAD_DIM)
    tab_r = _rope_tables(n_lat, n_ctx, C_ROPE)

    cond8 = jnp.zeros((8, d), F32).at[0].set(c[0]).at[1].set(c_ctx)
    mods = modulation_all(cond8, ada_w, ada_b).reshape(DEPTH, 8, N_MOD, d)

    xs = jnp.concatenate([x[0], ctx[0]], axis=0)
    h = modulate(xs, _vec_pack(mods[0], None, None, None, mods[0], 0, 1), n_lat)

    for i in range(DEPTH):
        kind = i % 4
        need_ctx = i < DEPTH - 1
        rows = t if need_ctx else n_lat
        att_kw = dict(n_lat=n_lat, n_ctx=n_ctx, with_ctx_q=need_ctx)
        if kind == 0:
            w = win_w_qkv[0]
            scale = HEAD_DIM ** -0.5
            q = project_heads(h, w, tab_h, None, rows=rows, col0=0, n_cols=d, half=32, scale=scale, name="win_q")
            k = project_heads(h, w, tab_h, None, rows=t, col0=d, n_cols=n_kv * HEAD_DIM, half=32, scale=1.0,
                              name="win_k")
            v = matmul(h, w, rows=t, col0=d + n_kv * HEAD_DIM, n_cols=n_kv * HEAD_DIM, out_dtype=BF16, name="win_v")
            o = window_attention(q, k, v, win_sink[0], n_heads_kv=n_kv, G=grp, **att_kw)
            w_o = win_w_o[0]
        elif kind == 1:
            w = qkn_w_qkv[0]
            scale = HEAD_DIM ** -0.5 * LOG2E
            q = project_heads(h, w, tab_h, qkn_q_gain[0].reshape(1, HEAD_DIM), rows=rows, col0=0, n_cols=d, half=32,
                              scale=scale, norm=True, name="qkn_q")
            k = project_heads(h, w, tab_h, qkn_k_gain[0].reshape(1, HEAD_DIM), rows=t, col0=d,
                              n_cols=n_kv * HEAD_DIM, half=32, scale=1.0, norm=True, name="qkn_k")
            vt = matmul(h, w, rows=t, col0=d + n_kv * HEAD_DIM, n_cols=n_kv * HEAD_DIM, out_dtype=BF16,
                        transpose_out=True, name="qkn_vt")
            o = flash_attention(q, k, vt, n_groups=n_kv, n_kv=1, G=grp, dq=HEAD_DIM, name="qkn_attn", **att_kw)
            w_o = qkn_w_o[0]
        elif kind == 2:
            w_a = mla_w_a[0]
            q_rank = mla_q_gain.shape[1]
            kv_rank = mla_kv_gain.shape[1]
            scale = (C_NOPE + C_ROPE) ** -0.5 * LOG2E
            cq = matmul(h, w_a, rows=rows, col0=0, n_cols=q_rank, out_dtype=F32, name="mla_cq")
            ckv = matmul(h, w_a, rows=t, col0=q_rank, n_cols=kv_rank, out_dtype=F32, name="mla_ckv")
            w_pe = jnp.pad(w_a[:, q_rank + kv_rank:], ((0, 0), (0, LANES - C_ROPE)))
            k_pe = project_heads(h, w_pe, tab_r, None, rows=t, col0=0, n_cols=LANES, half=16, scale=1.0,
                                 name="mla_kpe")
            w_qb = mla_w_qb[0].reshape(q_rank, n_heads, C_NOPE + C_ROPE)
            w_qb = jnp.pad(w_qb, ((0, 0), (0, 0), (0, 2 * LANES - C_NOPE - C_ROPE))).reshape(q_rank, n_heads * 2 * LANES)
            w_kvb = mla_w_kvb[0].reshape(kv_rank, n_heads, C_NOPE + C_V)
            w_kn = w_kvb[:, :, :C_NOPE].reshape(kv_rank, n_heads * C_NOPE)
            w_v = w_kvb[:, :, C_NOPE:].reshape(kv_rank, n_heads * C_V)
            tm = _pick(rows, (1024, 768, 512, 256, 128))
            q_epi = functools.partial(_head_epilogue, half=16, scale=scale, norm=False, rope_chunks=(1, 2),
                                      split_halves=False)
            q = rms_matmul(cq, mla_q_gain[0], w_qb, rows=rows, n_cols=n_heads * 2 * LANES, out_dtype=BF16,
                           epilogue=lambda acc, cs, sn: q_epi(acc, cs, sn, None),
                           extras=tab_r,
                           extra_specs=(pl.BlockSpec((tm, LANES), lambda j, i: (i, 0)),
                                        pl.BlockSpec((tm, LANES), lambda j, i: (i, 0))), name="mla_q")
            k_nope = rms_matmul(ckv, mla_kv_gain[0], w_kn, rows=t, n_cols=n_heads * C_NOPE, out_dtype=BF16,
                                name="mla_kn")
            vt = rms_matmul(ckv, mla_kv_gain[0], w_v, rows=t, n_cols=n_heads * C_V, out_dtype=BF16,
                            transpose_out=True, name="mla_vt")
            hp = 4
            o = flash_attention(q, k_nope, vt, k2=k_pe, n_groups=n_heads // hp, n_kv=hp, G=1, dq=2 * LANES,
                                name="mla_attn", **att_kw)
            w_o = mla_w_o[0]
        else:
            w = diff_w_qkv[0]
            lambda_init = 0.8 - 0.6 * math.exp(-0.3 * i)
            scale = DF_HEAD ** -0.5 * LOG2E
            n_dh = d // (2 * DF_HEAD)
            q = project_heads(h, w, tab_r, None, rows=rows, col0=0, n_cols=d, half=16, scale=scale,
                              split_halves=True, name="diff_q")
            k = project_heads(h, w, tab_r, None, rows=t, col0=d, n_cols=d, half=16, scale=1.0, name="diff_k")
            vt = matmul(h, w, rows=t, col0=2 * d, n_cols=d, out_dtype=BF16, transpose_out=True, name="diff_vt")
            hp = 2
            o = flash_attention(q, k, vt, diff=(diff_lambda[0], diff_subln[0], lambda_init), n_groups=n_dh // hp,
                                n_kv=hp, G=2, dq=LANES, name="diff_attn", **att_kw)
            w_o = diff_w_o[0]

        a = matmul(o, w_o, rows=rows, col0=0, n_cols=d, out_dtype=F32, name="attn_out")
        w_r = jnp.concatenate([moe_w_expert[i], moe_w_group[i],
                               jnp.zeros((d, LANES - N_EXPERTS - N_GROUPS), F32)], axis=1)
        b_r = jnp.concatenate([moe_b_expert[i], moe_b_group[i],
                               jnp.zeros((LANES - N_EXPERTS - N_GROUPS,), F32)]).reshape(1, LANES)
        xs, hx = ln_router(xs, a, _vec_pack(mods[i], 2, ln_g[i, 0], ln_b[i, 0], mods[i], 3, 4), w_r, b_r,
                           n_lat, rows)
        plan = moe_plan(hx, d, MOE_TILE)
        y_slots = moe_experts(hx, plan, moe_w_gate_up, moe_w_down, i, MOE_TILE)
        if i + 1 < DEPTH:
            xs, h = combine_ln(xs, y_slots, plan[3], _vec_pack(mods[i], 5, ln_g[i, 1], ln_b[i, 1], mods[i + 1], 0, 1),
                               n_lat, rows)
        else:
            (xs,) = combine_ln(xs, y_slots, plan[3], _vec_pack(mods[i], 5, ln_g[i, 1], ln_b[i, 1], None, 0, 1),
                               n_lat, rows, emit_h=False)
    return xs[:n_lat].reshape(b, n_lat, d)
```

```python
import functools
import math

import jax
import jax.numpy as jnp
from jax import lax
from jax.experimental import pallas as pl
from jax.experimental.pallas import tpu as pltpu

F32 = jnp.float32
BF16 = jnp.bfloat16

DEPTH = 4
GRID_W = 64
HEAD_DIM = 128
ROPE_BASE = 10000.0
EPS = 1e-6
NEG_INF = -1e30
N_MOD = 6
WINDOW = 128
C_NOPE = 128
C_ROPE = 64
C_V = 128
DF_HEAD = 64
N_GROUPS = 4
EXPERTS_PER_GROUP = 6
N_EXPERTS = N_GROUPS * EXPERTS_PER_GROUP
ALPHA = (2.0 * DEPTH) ** 0.25
LOG2E = math.log2(math.e)
LANES = 128
ITEMS_PER_STEP = 8
KEY_CHUNK = 512
MOE_TILE = 256
QUERY_COLS = 1024
ONES_ROWS = 16
VMEM_LIMIT_BYTES = 56 * 1024 * 1024


def _pick(n, cands):
    for c in cands:
        if n % c == 0:
            return c
    raise ValueError(f"no tile in {cands} divides {n}")


def _params(*sem):
    return pltpu.CompilerParams(dimension_semantics=sem, vmem_limit_bytes=VMEM_LIMIT_BYTES)


def _mod_kernel(cond_ref, w_ref, b_ref, o_ref):
    a = cond_ref[...]
    a = (a * jax.nn.sigmoid(a)).astype(BF16)
    o_ref[0] = jnp.dot(a, w_ref[0].astype(BF16), preferred_element_type=F32) + b_ref[0]


def modulation_all(cond8, ada_w, ada_b):
    depth, d, n = ada_w.shape
    tn = _pick(n, (512, 256, 128))
    return pl.pallas_call(
        _mod_kernel,
        grid=(depth, n // tn),
        in_specs=[pl.BlockSpec((8, d), lambda l, j: (0, 0)),
                  pl.BlockSpec((1, d, tn), lambda l, j: (l, 0, j)),
                  pl.BlockSpec((1, 1, tn), lambda l, j: (l, 0, j))],
        out_specs=pl.BlockSpec((1, 8, tn), lambda l, j: (l, 0, j)),
        out_shape=jax.ShapeDtypeStruct((depth, 8, n), F32),
        compiler_params=_params("parallel", "parallel"),
        name="modulation",
    )(cond8, ada_w, ada_b.reshape(depth, 1, n))


def _modulate_kernel(x_ref, v_ref, h_ref):
    v = v_ref[0]
    h_ref[...] = (x_ref[...] * (1.0 + v[4:5]) + v[3:4]).astype(h_ref.dtype)


def _deepnorm(x, y, v):
    z = ALPHA * x + v[0:1] * y
    mu = jnp.mean(z, axis=-1, keepdims=True)
    zc = z - mu
    var = jnp.mean(zc * zc, axis=-1, keepdims=True)
    xn = zc * lax.rsqrt(var + EPS) * v[1:2] + v[2:3]
    return xn, xn * (1.0 + v[4:5]) + v[3:4]


def _route(logits):
    lane = lax.broadcasted_iota(jnp.int32, logits.shape, 1)
    big = jnp.int32(1 << 20)
    is_group = (lane >= N_EXPERTS) & (lane < N_EXPERTS + N_GROUPS)
    gl = jnp.where(is_group, logits, NEG_INF)
    gmax = jnp.max(gl, axis=1, keepdims=True)
    g_idx = jnp.min(jnp.where(gl == gmax, lane, big), axis=1, keepdims=True) - N_EXPERTS
    g_w = 1.0 / jnp.sum(jnp.exp(gl - gmax), axis=1, keepdims=True)
    lo = g_idx * EXPERTS_PER_GROUP
    el = jnp.where((lane >= lo) & (lane < lo + EXPERTS_PER_GROUP), logits, NEG_INF)
    v1 = jnp.max(el, axis=1, keepdims=True)
    i1 = jnp.min(jnp.where(el == v1, lane, big), axis=1, keepdims=True)
    el2 = jnp.where(lane == i1, NEG_INF, el)
    v2 = jnp.max(el2, axis=1, keepdims=True)
    i2 = jnp.min(jnp.where(el2 == v2, lane, big), axis=1, keepdims=True)
    e2 = jnp.exp(v2 - v1)
    den = 1.0 + e2
    w1 = (1.0 / den) * g_w
    w2 = (e2 / den) * g_w
    comb = jnp.where(lane == i1, w1, 0.0) + jnp.where(lane == i2, w2, 0.0)
    comb = jnp.where(lane == N_EXPERTS, i1.astype(F32), comb)
    return jnp.where(lane == N_EXPERTS + 1, i2.astype(F32), comb)


def _ln_router_kernel(x_ref, y_ref, v_ref, wr_ref, br_ref, xo_ref, hx_ref):
    d = x_ref.shape[1]
    xn, h = _deepnorm(x_ref[...], y_ref[...], v_ref[0])
    xo_ref[...] = xn
    logits = jnp.dot(h.astype(BF16), wr_ref[...].astype(BF16), preferred_element_type=F32) + br_ref[...]
    hx_ref[:, :d] = h
    hx_ref[:, d:] = _route(logits)


def ln_router(x, y, vecs, w_r, b_r, n_lat, rows):
    d = x.shape[1]
    tr = _pick(math.gcd(rows, n_lat), (256, 128, 64, 32, 16, 8))
    row, vec = _row_specs(tr, d, n_lat // tr)
    return pl.pallas_call(
        _ln_router_kernel, grid=(rows // tr,),
        in_specs=[row, row, vec, pl.BlockSpec((d, LANES), lambda i: (0, 0)),
                  pl.BlockSpec((1, LANES), lambda i: (0, 0))],
        out_specs=[row, pl.BlockSpec((tr, d + LANES), lambda i: (i, 0))],
        out_shape=[jax.ShapeDtypeStruct((rows, d), F32), jax.ShapeDtypeStruct((rows, d + LANES), F32)],
        compiler_params=_params("parallel"), name="ln_router",
    )(x, y, vecs, w_r, b_r)


def _row_specs(tr, d, n_lat_tiles):
    row = pl.BlockSpec((tr, d), lambda i: (i, 0))
    vec = pl.BlockSpec((1, 8, d), lambda i: ((i >= n_lat_tiles).astype(jnp.int32), 0, 0))
    return row, vec


def modulate(x, vecs, n_lat):
    t, d = x.shape
    tr = _pick(math.gcd(t, n_lat), (256, 128, 64, 32, 16, 8))
    row, vec = _row_specs(tr, d, n_lat // tr)
    return pl.pallas_call(
        _modulate_kernel, grid=(t // tr,), in_specs=[row, vec], out_specs=row,
        out_shape=jax.ShapeDtypeStruct((t, d), BF16),
        compiler_params=_params("parallel"), name="modulate",
    )(x, vecs)


def _row_copy(src_hbm, row, dst_vmem, dst_row, sem):
    return pltpu.make_async_copy(src_hbm.at[pl.ds(row, 1)], dst_vmem.at[pl.ds(dst_row, 1)], sem)


def _combine_ln_kernel(pos_ref, x_ref, y_hbm, v_ref, xo_ref, *rest, tr, n_tiles):
    h_refs, (ybuf, sems) = rest[:-2], rest[-2:]
    i = pl.program_id(0)
    slot = i % 2

    def issue(tile, to_slot):
        def body(r, carry):
            t = tile * tr + r
            _row_copy(y_hbm, pos_ref[2 * t], ybuf.at[to_slot], r, sems.at[to_slot]).start()
            _row_copy(y_hbm, pos_ref[2 * t + 1], ybuf.at[to_slot], tr + r, sems.at[to_slot]).start()
            return carry
        lax.fori_loop(0, tr, body, 0, unroll=8)

    @pl.when(i == 0)
    def _():
        issue(0, 0)

    @pl.when(i + 1 < n_tiles)
    def _():
        issue(i + 1, 1 - slot)

    pltpu.make_async_copy(y_hbm.at[pl.ds(0, 2 * tr)], ybuf.at[slot], sems.at[slot]).wait()
    y = ybuf[slot, :tr] + ybuf[slot, tr:]
    xn, h = _deepnorm(x_ref[...], y, v_ref[0])
    xo_ref[...] = xn
    if h_refs:
        h_refs[0][...] = h.astype(h_refs[0].dtype)


def combine_ln(x, y_sorted, pair_pos, vecs, n_lat, rows, emit_h=True):
    d = x.shape[1]
    tr = _pick(math.gcd(rows, n_lat), (128, 64, 32, 16, 8))
    n_tiles = rows // tr
    n_lat_tiles = n_lat // tr
    row = pl.BlockSpec((tr, d), lambda i, pos: (i, 0))
    vec = pl.BlockSpec((1, 8, d), lambda i, pos: ((i >= n_lat_tiles).astype(jnp.int32), 0, 0))
    out_shape = [jax.ShapeDtypeStruct((rows, d), F32)]
    out_specs = [row]
    if emit_h:
        out_shape.append(jax.ShapeDtypeStruct((rows, d), BF16))
        out_specs.append(row)
    return pl.pallas_call(
        functools.partial(_combine_ln_kernel, tr=tr, n_tiles=n_tiles),
        grid_spec=pltpu.PrefetchScalarGridSpec(
            num_scalar_prefetch=1, grid=(n_tiles,),
            in_specs=[row, pl.BlockSpec(memory_space=pl.ANY), vec],
            out_specs=out_specs,
            scratch_shapes=[pltpu.VMEM((2, 2 * tr, d), F32), pltpu.SemaphoreType.DMA((2,))]),
        out_shape=out_shape, compiler_params=_params("arbitrary"), name="combine_ln",
    )(pair_pos, x, y_sorted, vecs)


def _mm_kernel(*refs, n_extra, prologue, epilogue, transpose_out):
    a_ref, w_ref = refs[0], refs[1]
    extras = refs[2:2 + n_extra]
    o_ref, wb_ref = refs[2 + n_extra], refs[3 + n_extra]

    @pl.when(pl.program_id(1) == 0)
    def _():
        wb_ref[...] = w_ref[...].astype(BF16)

    a = a_ref[...]
    if prologue is not None:
        a = prologue(a, *extras)
    acc = jnp.dot(a.astype(BF16), wb_ref[...], preferred_element_type=F32)
    if epilogue is not None:
        acc = epilogue(acc, *extras)
    if transpose_out:
        acc = acc.T
    o_ref[...] = acc.astype(o_ref.dtype)


def matmul(a, w, *, rows, col0, n_cols, out_dtype, prologue=None, epilogue=None, extras=(),
           extra_specs=(), out_mult=1, transpose_out=False, name="matmul"):
    k = a.shape[1]
    tm = _pick(rows, (1024, 768, 512, 256, 128))
    tn = _pick(math.gcd(n_cols, col0) if col0 else n_cols, (512, 256, 128))
    cb0 = col0 // tn
    kern = functools.partial(_mm_kernel, n_extra=len(extras), prologue=prologue, epilogue=epilogue,
                             transpose_out=transpose_out)
    if transpose_out:
        out_spec = pl.BlockSpec((tn * out_mult, tm), lambda j, i: (j, i))
        out_shape = jax.ShapeDtypeStruct((n_cols * out_mult, rows), out_dtype)
    else:
        out_spec = pl.BlockSpec((tm, tn * out_mult), lambda j, i: (i, j))
        out_shape = jax.ShapeDtypeStruct((rows, n_cols * out_mult), out_dtype)
    return pl.pallas_call(
        kern,
        grid=(n_cols // tn, rows // tm),
        in_specs=[pl.BlockSpec((tm, k), lambda j, i: (i, 0)),
                  pl.BlockSpec((k, tn), lambda j, i: (0, cb0 + j)),
                  *extra_specs],
        out_specs=out_spec,
        out_shape=out_shape,
        scratch_shapes=[pltpu.VMEM((k, tn), BF16)],
        compiler_params=_params("parallel", "arbitrary"),
        name=name,
    )(a, w, *extras)


def _swap_pairs(x, half):
    n = x.shape[-1]
    lane = lax.broadcasted_iota(jnp.int32, x.shape, x.ndim - 1)
    fwd = pltpu.roll(x, n - half, axis=x.ndim - 1)
    bwd = pltpu.roll(x, half, axis=x.ndim - 1)
    return jnp.where((lane % (2 * half)) < half, fwd, bwd)


def _rope(x, cos, sin_signed, half):
    return x * cos + _swap_pairs(x, half) * sin_signed


def _rms(x, gain):
    return x * lax.rsqrt(jnp.mean(x * x, axis=-1, keepdims=True) + EPS) * gain


def _head_epilogue(acc, cos_ref, sin_ref, gain_ref, *, half, scale, norm, rope_chunks, split_halves):
    outs = []
    for c in range(acc.shape[1] // LANES):
        x = acc[:, c * LANES:(c + 1) * LANES]
        if norm:
            x = _rms(x, gain_ref[...])
        if rope_chunks is None or (c % rope_chunks[1]) == rope_chunks[0]:
            x = _rope(x, cos_ref[...], sin_ref[...], half)
        if scale != 1.0:
            x = x * scale
        if split_halves:
            lane = lax.broadcasted_iota(jnp.int32, x.shape, 1)
            outs.append(jnp.where(lane < LANES // 2, x, 0.0))
            outs.append(jnp.where(lane >= LANES // 2, x, 0.0))
        else:
            outs.append(x)
    return jnp.concatenate(outs, axis=1) if len(outs) > 1 else outs[0]


def project_heads(a, w, tables, gain, *, rows, col0, n_cols, half, scale, norm=False,
                  rope_chunks=None, split_halves=False, name="proj"):
    cos, sin = tables
    tm = _pick(rows, (1024, 768, 512, 256, 128))
    if gain is None:
        gain = jnp.ones((1, LANES), F32)
    epi = functools.partial(_head_epilogue, half=half, scale=scale, norm=norm, rope_chunks=rope_chunks,
                            split_halves=split_halves)
    return matmul(
        a, w, rows=rows, col0=col0, n_cols=n_cols, out_dtype=BF16, epilogue=epi,
        extras=(cos, sin, gain),
        extra_specs=(pl.BlockSpec((tm, LANES), lambda j, i: (i, 0)),
                     pl.BlockSpec((tm, LANES), lambda j, i: (i, 0)),
                     pl.BlockSpec((1, LANES), lambda j, i: (0, 0))),
        out_mult=2 if split_halves else 1, name=name)


def rms_matmul(a, gain, w, *, rows, n_cols, out_dtype, epilogue=None, extras=(), extra_specs=(),
               transpose_out=False, name="rms_mm"):
    k = a.shape[1]
    pro = lambda a_t, g_ref, *rest: _rms(a_t, g_ref[...])
    epi = None if epilogue is None else (lambda acc, g_ref, *rest: epilogue(acc, *rest))
    return matmul(a, w, rows=rows, col0=0, n_cols=n_cols, out_dtype=out_dtype, prologue=pro, epilogue=epi,
                  extras=(gain.reshape(1, k), *extras),
                  extra_specs=(pl.BlockSpec((1, k), lambda j, i: (0, 0)), *extra_specs),
                  transpose_out=transpose_out, name=name)


def _kq(k, q):
    return lax.dot_general(k, q, (((1,), (1,)), ((), ())), preferred_element_type=F32)


def _col_reduce(x, op):
    rows, n = x.shape
    parts = 8 if rows % 64 == 0 else 1
    if parts > 1:
        x = op(x.reshape(parts, rows // parts, n), axis=1)
    return op(x, axis=0, keepdims=True)


def _flash_kernel(*refs, n_kv, G, dq, has_k2, diff, nq_lat, nk, kc, qcols):
    refs = list(refs)
    if diff is not None:
        lam_ref, sub_ref = refs.pop(0), refs.pop(0)
    q_ref, k_ref = refs.pop(0), refs.pop(0)
    k2_ref = refs.pop(0) if has_k2 else None
    vt_ref, kx_ref = refs.pop(0), refs.pop(0)
    kx2_ref = refs.pop(0) if has_k2 else None
    vxt_ref = refs.pop(0)
    o_ref, m_scr, acc_scr = refs
    i, j = pl.program_id(1), pl.program_id(2)
    tq = q_ref.shape[0]

    def folded_q(s):
        parts = [q_ref[:, (s * G + g) * dq:(s * G + g + 1) * dq] for g in range(G)]
        return jnp.concatenate(parts, axis=0) if G > 1 else parts[0]

    def scores(qs, k_r, k2_r, s, rows, cols):
        k = k_r[rows, s * LANES:(s + 1) * LANES]
        if k2_r is not None:
            k = jnp.concatenate([k, k2_r[rows, :]], axis=1)
        return _kq(k, qs[s][cols])

    def update(s, cols, st, vt):
        m_prev = m_scr[s, :, cols]
        m_new = jnp.maximum(m_prev, _col_reduce(st, jnp.max))
        p = jnp.exp2((st - m_new).astype(BF16))
        alpha = jnp.exp2(m_prev - m_new)
        vt1 = jnp.concatenate([vt, jnp.ones((ONES_ROWS, vt.shape[1]), BF16)], axis=0)
        acc_scr[s, :, cols] = acc_scr[s, :, cols] * alpha + jnp.dot(vt1, p, preferred_element_type=F32)
        m_scr[s, :, cols] = m_new

    def run(k_r, k2_r, vt_r, row_slices):
        qs = [folded_q(s) for s in range(n_kv)]
        col_slices = [slice(c0, c0 + qcols) for c0 in range(0, G * tq, qcols)]
        items = [(s, rows, cols) for rows in row_slices for s in range(n_kv) for cols in col_slices]
        st_next = scores(qs, k_r, k2_r, *items[0])
        for t, (s, rows, cols) in enumerate(items):
            st = st_next
            if t + 1 < len(items):
                st_next = scores(qs, k_r, k2_r, *items[t + 1])
            update(s, cols, st, vt_r[s * LANES:(s + 1) * LANES, rows])

    @pl.when(j == 0)
    def _():
        m_scr[...] = jnp.full(m_scr.shape, NEG_INF, F32)
        acc_scr[...] = jnp.zeros(acc_scr.shape, F32)
        run(kx_ref, kx2_ref, vxt_ref, [slice(None)])

    @pl.when(i < nq_lat)
    def _():
        run(k_ref, k2_ref, vt_ref, [slice(cc * kc, (cc + 1) * kc) for cc in range(k_ref.shape[0] // kc)])

    def normalized(s):
        return acc_scr[s, :LANES] / acc_scr[s, LANES:LANES + 1]

    @pl.when(j == nk - 1)
    def _():
        if diff is None:
            for s in range(n_kv):
                o = normalized(s)
                for g in range(G):
                    c = s * G + g
                    o_ref[:, c * LANES:(c + 1) * LANES] = o[:, g * tq:(g + 1) * tq].T.astype(o_ref.dtype)
        else:
            lv = lam_ref[...]
            lam = (jnp.exp(jnp.sum(lv[0:1] * lv[1:2], axis=1, keepdims=True))
                   - jnp.exp(jnp.sum(lv[2:3] * lv[3:4], axis=1, keepdims=True)) + diff)
            for s in range(n_kv):
                o = normalized(s)
                o = o[:, :tq] - lam * o[:, tq:]
                o = _rms(o.T, sub_ref[...]) * (1.0 - diff)
                o_ref[:, s * LANES:(s + 1) * LANES] = o.astype(o_ref.dtype)


def flash_attention(q, k, vt, *, tq, n_lat, n_ctx, with_ctx_q, name, **kw):
    kw.update(n_lat=n_lat, n_ctx=n_ctx)
    if tq == n_ctx or not with_ctx_q:
        return _flash_call(q, k, vt, tq=tq, qb0=0, nq_lat=n_lat // tq, n_ctx_q=1 if with_ctx_q else 0,
                           name=name, **kw)
    o_lat = _flash_call(q, k, vt, tq=tq, qb0=0, nq_lat=n_lat // tq, n_ctx_q=0, name=name, **kw)
    o_ctx = _flash_call(q, k, vt, tq=n_ctx, qb0=n_lat // n_ctx, nq_lat=0, n_ctx_q=1, name=name + "_ctx", **kw)
    return jnp.concatenate([o_lat, o_ctx], axis=0)


def _flash_call(q, k, vt, *, k2=None, diff=None, n_groups, n_kv, G, dq, n_lat, n_ctx, tq, qb0, nq_lat,
                n_ctx_q, name):
    nqb = nq_lat + n_ctx_q
    tk = _pick(n_lat, (ITEMS_PER_STEP * KEY_CHUNK // n_kv, 1024, 512, 256, 128))
    nk = n_lat // tk if nq_lat else 1
    kc = min(tk, KEY_CHUNK)
    cb = n_lat // n_ctx
    kw = n_kv * LANES
    n_slots = n_kv * G
    ow = (n_kv if diff is not None else n_slots) * LANES

    def jmap(i, j):
        return jnp.where(i < nq_lat, j, 0)

    args, specs = [], []
    if diff is not None:
        lam_vecs, subln, lambda_init = diff
        args += [lam_vecs, subln.reshape(1, LANES)]
        specs += [pl.BlockSpec(lam_vecs.shape, lambda h, i, j: (0, 0)),
                  pl.BlockSpec((1, LANES), lambda h, i, j: (0, 0))]
    args += [q, k]
    specs += [pl.BlockSpec((tq, n_slots * dq), lambda h, i, j: (qb0 + i, h)),
              pl.BlockSpec((tk, kw), lambda h, i, j: (jmap(i, j), h))]
    if k2 is not None:
        args.append(k2)
        specs.append(pl.BlockSpec((tk, LANES), lambda h, i, j: (jmap(i, j), 0)))
    args += [vt, k]
    specs += [pl.BlockSpec((kw, tk), lambda h, i, j: (h, jmap(i, j))),
              pl.BlockSpec((n_ctx, kw), lambda h, i, j: (cb, h))]
    if k2 is not None:
        args.append(k2)
        specs.append(pl.BlockSpec((n_ctx, LANES), lambda h, i, j: (cb, 0)))
    args.append(vt)
    specs.append(pl.BlockSpec((kw, n_ctx), lambda h, i, j: (h, cb)))
    kern = functools.partial(_flash_kernel, n_kv=n_kv, G=G, dq=dq, has_k2=k2 is not None,
                             diff=None if diff is None else diff[2],
                             nq_lat=nq_lat, nk=nk, kc=kc, qcols=min(G * tq, QUERY_COLS))
    return pl.pallas_call(
        kern,
        grid=(n_groups, nqb, nk),
        in_specs=specs,
        out_specs=pl.BlockSpec((tq, ow), lambda h, i, j: (i, h)),
        out_shape=jax.ShapeDtypeStruct((nqb * tq, n_groups * ow), BF16),
        scratch_shapes=[pltpu.VMEM((n_kv, 1, G * tq), F32),
                        pltpu.VMEM((n_kv, LANES + ONES_ROWS, G * tq), F32)],
        compiler_params=_params("parallel", "parallel", "arbitrary"),
        name=name,
    )(*args)


def _qk(q, k):
    return lax.dot_general(q, k, (((1,), (1,)), ((), ())), preferred_element_type=F32)


def _window_kernel(sink_ref, q_ref, kp_ref, kc_ref, kn_ref, kx_ref, vp_ref, vc_ref, vn_ref, vx_ref,
                   o_ref, *, G, tq, nq):
    h, i = pl.program_id(0), pl.program_id(1)
    k = jnp.concatenate([kp_ref[...], kc_ref[...], kn_ref[...], kx_ref[...]], axis=0)
    v = jnp.concatenate([vp_ref[...], vc_ref[...], vn_ref[...], vx_ref[...]], axis=0)
    n_band = tq + 2 * WINDOW
    r = lax.broadcasted_iota(jnp.int32, (tq, k.shape[0]), 0)
    c = lax.broadcasted_iota(jnp.int32, (tq, k.shape[0]), 1)
    rel = c - WINDOW - r
    kpos = i * tq - WINDOW + c
    band = (jnp.abs(rel) <= WINDOW) & (kpos >= 0) & (kpos < nq * tq) & (i * tq + r < nq * tq)
    mask = band | (c >= n_band)
    for g in range(G):
        s = _qk(q_ref[:, g * HEAD_DIM:(g + 1) * HEAD_DIM], k)
        s = jnp.where(mask, s, NEG_INF)
        sk = sink_ref[h * G + g]
        m = jnp.maximum(jnp.max(s, axis=1, keepdims=True), sk)
        p = jnp.exp(s - m)
        l = jnp.sum(p, axis=1, keepdims=True) + jnp.exp(sk - m)
        o = jnp.dot(p.astype(BF16), v, preferred_element_type=F32) / l
        o_ref[:, g * HEAD_DIM:(g + 1) * HEAD_DIM] = o.astype(o_ref.dtype)


def window_attention(q, k, v, sink, *, n_heads_kv, G, n_lat, n_ctx, with_ctx_q):
    tq = 2 * WINDOW
    assert n_ctx == tq
    nq = n_lat // tq
    nqb = nq + (1 if with_ctx_q else 0)
    nb = n_lat // WINDOW
    cb = n_lat // n_ctx
    w = WINDOW

    def prev_map(h, i):
        return (jnp.maximum(2 * i - 1, 0), h)

    def next_map(h, i):
        return (jnp.minimum(2 * i + 2, nb - 1), h)

    kv_specs = [pl.BlockSpec((w, HEAD_DIM), prev_map),
                pl.BlockSpec((tq, HEAD_DIM), lambda h, i: (i, h)),
                pl.BlockSpec((w, HEAD_DIM), next_map),
                pl.BlockSpec((n_ctx, HEAD_DIM), lambda h, i: (cb, h))]
    kern = functools.partial(_window_kernel, G=G, tq=tq, nq=nq)
    return pl.pallas_call(
        kern,
        grid=(n_heads_kv, nqb),
        in_specs=[pl.BlockSpec(memory_space=pltpu.SMEM),
                  pl.BlockSpec((tq, G * HEAD_DIM), lambda h, i: (i, h)),
                  *kv_specs, *kv_specs],
        out_specs=pl.BlockSpec((tq, G * HEAD_DIM), lambda h, i: (i, h)),
        out_shape=jax.ShapeDtypeStruct((nqb * tq, n_heads_kv * G * HEAD_DIM), BF16),
        compiler_params=_params("parallel", "parallel"),
        name="window_attention",
    )(sink, q, k, k, k, k, v, v, v, v)


def moe_plan(hx, d, tm):
    rows = hx.shape[0]
    n_pairs = 2 * rows
    n_tiles = n_pairs // tm + N_EXPERTS
    e_flat = hx[:, d + N_EXPERTS:d + N_EXPERTS + 2].astype(jnp.int32).reshape(n_pairs)
    order = jnp.argsort(e_flat, stable=True).astype(jnp.int32)
    rank = jnp.argsort(order).astype(jnp.int32)
    counts = jnp.sum(e_flat[:, None] == jnp.arange(N_EXPERTS, dtype=jnp.int32)[None, :], axis=0, dtype=jnp.int32)
    tiles_e = (counts + tm - 1) // tm
    tile_end = jnp.cumsum(tiles_e)
    tile_start = tile_end - tiles_e
    seg_start = jnp.cumsum(counts) - counts
    pair_slot = tile_start[e_flat] * tm + rank - seg_start[e_flat]
    n_used = tile_end[-1]
    tile_ids = jnp.minimum(jnp.arange(n_tiles, dtype=jnp.int32), n_used - 1)
    tile_expert = jnp.sum(tile_ids[:, None] >= tile_end[None, :], axis=1, dtype=jnp.int32)
    e_slot = jnp.repeat(tile_expert, tm)
    r_slot = jnp.arange(n_tiles * tm, dtype=jnp.int32) - tile_start[e_slot] * tm
    valid = (r_slot < counts[e_slot]) & (jnp.arange(n_tiles * tm, dtype=jnp.int32) < n_used * tm)
    src_pair = order[jnp.clip(seg_start[e_slot] + r_slot, 0, n_pairs - 1)]
    src_tok = jnp.where(valid, src_pair // 2, 0)
    return src_tok, tile_expert, n_used.reshape(1), pair_slot


def _moe_experts_kernel(te_ref, src_ref, n_ref, hx_hbm, wgu_ref, wd_ref, o_ref, xbuf, sems, wgu_bf, wd_bf,
                        *, tm, d):
    i = pl.program_id(0)
    n_used = n_ref[0]
    slot = i % 2

    def issue(tile, to_slot):
        def body(r, carry):
            _row_copy(hx_hbm, src_ref[tile * tm + r], xbuf.at[to_slot], r, sems.at[to_slot]).start()
            return carry
        lax.fori_loop(0, tm, body, 0, unroll=8)

    @pl.when(i == 0)
    def _():
        issue(0, 0)

    @pl.when(i + 1 < n_used)
    def _():
        issue(i + 1, 1 - slot)

    @pl.when(i < n_used)
    def _():
        e = te_ref[i]

        @pl.when((i == 0) | (e != te_ref[jnp.maximum(i - 1, 0)]))
        def _():
            wgu_bf[...] = wgu_ref[0, 0].astype(BF16)
            wd_bf[...] = wd_ref[0, 0].astype(BF16)

        pltpu.make_async_copy(hx_hbm.at[pl.ds(0, tm)], xbuf.at[slot], sems.at[slot]).wait()
        xg = xbuf[slot]
        gu = jnp.dot(xg[:, :d].astype(BF16), wgu_bf[...], preferred_element_type=F32)
        f = gu.shape[1] // 2
        gate, up = gu[:, :f], gu[:, f:]
        comb = xg[:, d:]
        lane = lax.broadcasted_iota(jnp.int32, comb.shape, 1)
        c = jnp.sum(jnp.where(lane == e, comb, 0.0), axis=1, keepdims=True)
        act = (gate * jax.nn.sigmoid(gate)) * up * c
        o_ref[...] = jnp.dot(act.astype(BF16), wd_bf[...], preferred_element_type=F32)

    @pl.when(i >= n_used)
    def _():
        o_ref[...] = jnp.zeros(o_ref.shape, o_ref.dtype)


def moe_experts(hx, plan, w_gate_up, w_down, layer, tm):
    src_tok, tile_expert, n_used, _ = plan
    d = hx.shape[1] - LANES
    f2 = w_gate_up.shape[3]
    n_tiles = tile_expert.shape[0]
    return pl.pallas_call(
        functools.partial(_moe_experts_kernel, tm=tm, d=d),
        grid_spec=pltpu.PrefetchScalarGridSpec(
            num_scalar_prefetch=3, grid=(n_tiles,),
            in_specs=[pl.BlockSpec(memory_space=pl.ANY),
                      pl.BlockSpec((1, 1, d, f2), lambda i, te, src, n: (layer, te[i], 0, 0)),
                      pl.BlockSpec((1, 1, f2 // 2, d), lambda i, te, src, n: (layer, te[i], 0, 0))],
            out_specs=pl.BlockSpec((tm, d), lambda i, te, src, n: (i, 0)),
            scratch_shapes=[pltpu.VMEM((2, tm, d + LANES), F32), pltpu.SemaphoreType.DMA((2,)),
                            pltpu.VMEM((d, f2), BF16), pltpu.VMEM((f2 // 2, d), BF16)]),
        out_shape=jax.ShapeDtypeStruct((n_tiles * tm, d), F32),
        compiler_params=_params("arbitrary"), name="moe_experts",
    )(tile_expert, src_tok, n_used, hx, w_gate_up, w_down)


def _rope_tables(n_lat, n_ctx, rot_dim):
    rows = n_lat // GRID_W
    r, col = jnp.meshgrid(jnp.arange(rows, dtype=F32), jnp.arange(GRID_W, dtype=F32), indexing="ij")
    pos = jnp.stack([r.reshape(-1), col.reshape(-1)], axis=-1)
    n_freq = rot_dim // 4
    inv_freq = ROPE_BASE ** (-jnp.arange(n_freq, dtype=F32) / n_freq)
    ang = pos[:, :, None] * inv_freq
    cos, sin = jnp.cos(ang), jnp.sin(ang)
    cos_full = jnp.concatenate([cos[:, 0], cos[:, 0], cos[:, 1], cos[:, 1]], axis=-1)
    sin_full = jnp.concatenate([-sin[:, 0], sin[:, 0], -sin[:, 1], sin[:, 1]], axis=-1)
    reps = LANES // rot_dim
    cos_full = jnp.tile(cos_full, (1, reps))
    sin_full = jnp.tile(sin_full, (1, reps))
    cos_full = jnp.concatenate([cos_full, jnp.ones((n_ctx, LANES), F32)], axis=0)
    sin_full = jnp.concatenate([sin_full, jnp.zeros((n_ctx, LANES), F32)], axis=0)
    return cos_full, sin_full


def _vec_pack(mods, gate_idx, ln_g, ln_b, next_mods, shift_idx, scale_idx):
    d = mods.shape[-1]
    z = jnp.zeros((2, d), F32)
    gate = mods[:2, gate_idx] if gate_idx is not None else z
    g = jnp.broadcast_to(ln_g, (2, d)) if ln_g is not None else z
    b = jnp.broadcast_to(ln_b, (2, d)) if ln_b is not None else z
    shift = next_mods[:2, shift_idx] if next_mods is not None else z
    scale = next_mods[:2, scale_idx] if next_mods is not None else z
    return jnp.stack([gate, g, b, shift, scale, z, z, z], axis=1)


def kernel(x, c, ctx, c_ctx, ada_w, ada_b, ln_g, ln_b, win_w_qkv, win_w_o, win_sink, qkn_w_qkv, qkn_q_gain, qkn_k_gain, qkn_w_o, mla_w_a, mla_q_gain, mla_kv_gain, mla_w_qb, mla_w_kvb, mla_w_o, diff_w_qkv, diff_lambda, diff_subln, diff_w_o, moe_w_group, moe_b_group, moe_w_expert, moe_b_expert, moe_w_gate_up, moe_w_down):
    b, n_lat, d = x.shape
    n_ctx = ctx.shape[1]
    assert b == 1
    t = n_lat + n_ctx
    n_heads = d // HEAD_DIM
    n_kv = n_heads // 4
    grp = n_heads // n_kv

    tab_h = _rope_tables(n_lat, n_ctx, HEAD_DIM)
    tab_r = _rope_tables(n_lat, n_ctx, C_ROPE)

    cond8 = jnp.zeros((8, d), F32).at[0].set(c[0]).at[1].set(c_ctx)
    mods = modulation_all(cond8, ada_w, ada_b).reshape(DEPTH, 8, N_MOD, d)

    xs = jnp.concatenate([x[0], ctx[0]], axis=0)
    h = modulate(xs, _vec_pack(mods[0], None, None, None, mods[0], 0, 1), n_lat)

    for i in range(DEPTH):
        kind = i % 4
        need_ctx = i < DEPTH - 1
        rows = t if need_ctx else n_lat
        att_kw = dict(n_lat=n_lat, n_ctx=n_ctx, with_ctx_q=need_ctx)
        if kind == 0:
            w = win_w_qkv[0]
            scale = HEAD_DIM ** -0.5
            q = project_heads(h, w, tab_h, None, rows=rows, col0=0, n_cols=d, half=32, scale=scale, name="win_q")
            k = project_heads(h, w, tab_h, None, rows=t, col0=d, n_cols=n_kv * HEAD_DIM, half=32, scale=1.0,
                              name="win_k")
            v = matmul(h, w, rows=t, col0=d + n_kv * HEAD_DIM, n_cols=n_kv * HEAD_DIM, out_dtype=BF16, name="win_v")
            o = window_attention(q, k, v, win_sink[0], n_heads_kv=n_kv, G=grp, **att_kw)
            w_o = win_w_o[0]
        elif kind == 1:
            w = qkn_w_qkv[0]
            scale = HEAD_DIM ** -0.5 * LOG2E
            q = project_heads(h, w, tab_h, qkn_q_gain[0].reshape(1, HEAD_DIM), rows=rows, col0=0, n_cols=d, half=32,
                              scale=scale, norm=True, name="qkn_q")
            k = project_heads(h, w, tab_h, qkn_k_gain[0].reshape(1, HEAD_DIM), rows=t, col0=d,
                              n_cols=n_kv * HEAD_DIM, half=32, scale=1.0, norm=True, name="qkn_k")
            vt = matmul(h, w, rows=t, col0=d + n_kv * HEAD_DIM, n_cols=n_kv * HEAD_DIM, out_dtype=BF16,
                        transpose_out=True, name="qkn_vt")
            o = flash_attention(q, k, vt, tq=n_ctx, n_groups=n_kv, n_kv=1, G=grp, dq=HEAD_DIM, name="qkn_attn",
                                **att_kw)
            w_o = qkn_w_o[0]
        elif kind == 2:
            w_a = mla_w_a[0]
            q_rank = mla_q_gain.shape[1]
            kv_rank = mla_kv_gain.shape[1]
            scale = (C_NOPE + C_ROPE) ** -0.5 * LOG2E
            cq = matmul(h, w_a, rows=rows, col0=0, n_cols=q_rank, out_dtype=F32, name="mla_cq")
            ckv = matmul(h, w_a, rows=t, col0=q_rank, n_cols=kv_rank, out_dtype=F32, name="mla_ckv")
            w_pe = jnp.pad(w_a[:, q_rank + kv_rank:], ((0, 0), (0, LANES - C_ROPE)))
            k_pe = project_heads(h, w_pe, tab_r, None, rows=t, col0=0, n_cols=LANES, half=16, scale=1.0,
                                 name="mla_kpe")
            w_qb = mla_w_qb[0].reshape(q_rank, n_heads, C_NOPE + C_ROPE)
            w_qb = jnp.pad(w_qb, ((0, 0), (0, 0), (0, 2 * LANES - C_NOPE - C_ROPE))).reshape(q_rank, n_heads * 2 * LANES)
            w_kvb = mla_w_kvb[0].reshape(kv_rank, n_heads, C_NOPE + C_V)
            w_kn = w_kvb[:, :, :C_NOPE].reshape(kv_rank, n_heads * C_NOPE)
            w_v = w_kvb[:, :, C_NOPE:].reshape(kv_rank, n_heads * C_V)
            tm = _pick(rows, (1024, 768, 512, 256, 128))
            q_epi = functools.partial(_head_epilogue, half=16, scale=scale, norm=False, rope_chunks=(1, 2),
                                      split_halves=False)
            q = rms_matmul(cq, mla_q_gain[0], w_qb, rows=rows, n_cols=n_heads * 2 * LANES, out_dtype=BF16,
                           epilogue=lambda acc, cs, sn: q_epi(acc, cs, sn, None),
                           extras=tab_r,
                           extra_specs=(pl.BlockSpec((tm, LANES), lambda j, i: (i, 0)),
                                        pl.BlockSpec((tm, LANES), lambda j, i: (i, 0))), name="mla_q")
            k_nope = rms_matmul(ckv, mla_kv_gain[0], w_kn, rows=t, n_cols=n_heads * C_NOPE, out_dtype=BF16,
                                name="mla_kn")
            vt = rms_matmul(ckv, mla_kv_gain[0], w_v, rows=t, n_cols=n_heads * C_V, out_dtype=BF16,
                            transpose_out=True, name="mla_vt")
            hp = 2
            o = flash_attention(q, k_nope, vt, tq=_pick(n_lat, (QUERY_COLS, n_ctx)), k2=k_pe,
                                n_groups=n_heads // hp, n_kv=hp, G=1, dq=2 * LANES, name="mla_attn", **att_kw)
            w_o = mla_w_o[0]
        else:
            w = diff_w_qkv[0]
            lambda_init = 0.8 - 0.6 * math.exp(-0.3 * i)
            scale = DF_HEAD ** -0.5 * LOG2E
            n_dh = d // (2 * DF_HEAD)
            q = project_heads(h, w, tab_r, None, rows=rows, col0=0, n_cols=d, half=16, scale=scale,
                              split_halves=True, name="diff_q")
            k = project_heads(h, w, tab_r, None, rows=t, col0=d, n_cols=d, half=16, scale=1.0, name="diff_k")
            vt = matmul(h, w, rows=t, col0=2 * d, n_cols=d, out_dtype=BF16, transpose_out=True, name="diff_vt")
            hp = 2
            o = flash_attention(q, k, vt, tq=_pick(n_lat, (QUERY_COLS // 2, n_ctx)),
                                diff=(diff_lambda[0], diff_subln[0], lambda_init), n_groups=n_dh // hp,
                                n_kv=hp, G=2, dq=LANES, name="diff_attn", **att_kw)
            w_o = diff_w_o[0]

        a = matmul(o, w_o, rows=rows, col0=0, n_cols=d, out_dtype=BF16, name="attn_out")
        w_r = jnp.concatenate([moe_w_expert[i], moe_w_group[i],
                               jnp.zeros((d, LANES - N_EXPERTS - N_GROUPS), F32)], axis=1)
        b_r = jnp.concatenate([moe_b_expert[i], moe_b_group[i],
                               jnp.zeros((LANES - N_EXPERTS - N_GROUPS,), F32)]).reshape(1, LANES)
        xs, hx = ln_router(xs, a, _vec_pack(mods[i], 2, ln_g[i, 0], ln_b[i, 0], mods[i], 3, 4), w_r, b_r,
                           n_lat, rows)
        plan = moe_plan(hx, d, MOE_TILE)
        y_slots = moe_experts(hx, plan, moe_w_gate_up, moe_w_down, i, MOE_TILE)
        if i + 1 < DEPTH:
            xs, h = combine_ln(xs, y_slots, plan[3], _vec_pack(mods[i], 5, ln_g[i, 1], ln_b[i, 1], mods[i + 1], 0, 1),
                               n_lat, rows)
        else:
            (xs,) = combine_ln(xs, y_slots, plan[3], _vec_pack(mods[i], 5, ln_g[i, 1], ln_b[i, 1], None, 0, 1),
                               n_lat, rows, emit_h=False)
    return xs[:n_lat].reshape(b, n_lat, d)
```

```python
import functools
import math

import jax
import jax.numpy as jnp
from jax import lax
from jax.experimental import pallas as pl
from jax.experimental.pallas import tpu as pltpu

F32 = jnp.float32
BF16 = jnp.bfloat16

DEPTH = 4
GRID_W = 64
HEAD_DIM = 128
ROPE_BASE = 10000.0
EPS = 1e-6
NEG_INF = -1e30
N_MOD = 6
WINDOW = 128
C_NOPE = 128
C_ROPE = 64
C_V = 128
DF_HEAD = 64
N_GROUPS = 4
EXPERTS_PER_GROUP = 6
N_EXPERTS = N_GROUPS * EXPERTS_PER_GROUP
ALPHA = (2.0 * DEPTH) ** 0.25
LOG2E = math.log2(math.e)
LANES = 128
ITEMS_PER_STEP = 8
KEY_CHUNK = 512
MOE_TILE = 256
QUERY_COLS = 1024
ONES_ROWS = 16
VMEM_LIMIT_BYTES = 56 * 1024 * 1024


def _pick(n, cands):
    for c in cands:
        if n % c == 0:
            return c
    raise ValueError(f"no tile in {cands} divides {n}")


def _params(*sem):
    return pltpu.CompilerParams(dimension_semantics=sem, vmem_limit_bytes=VMEM_LIMIT_BYTES)


def _mod_kernel(cond_ref, w_ref, b_ref, o_ref):
    a = cond_ref[...]
    a = (a * jax.nn.sigmoid(a)).astype(BF16)
    o_ref[0] = jnp.dot(a, w_ref[0].astype(BF16), preferred_element_type=F32) + b_ref[0]


def modulation_all(cond8, ada_w, ada_b):
    depth, d, n = ada_w.shape
    tn = _pick(n, (512, 256, 128))
    return pl.pallas_call(
        _mod_kernel,
        grid=(depth, n // tn),
        in_specs=[pl.BlockSpec((8, d), lambda l, j: (0, 0)),
                  pl.BlockSpec((1, d, tn), lambda l, j: (l, 0, j)),
                  pl.BlockSpec((1, 1, tn), lambda l, j: (l, 0, j))],
        out_specs=pl.BlockSpec((1, 8, tn), lambda l, j: (l, 0, j)),
        out_shape=jax.ShapeDtypeStruct((depth, 8, n), F32),
        compiler_params=_params("parallel", "parallel"),
        name="modulation",
    )(cond8, ada_w, ada_b.reshape(depth, 1, n))


def _modulate_kernel(x_ref, v_ref, h_ref):
    v = v_ref[0]
    h_ref[...] = (x_ref[...] * (1.0 + v[4:5]) + v[3:4]).astype(h_ref.dtype)


def _deepnorm(x, y, v):
    z = ALPHA * x + v[0:1] * y
    mu = jnp.mean(z, axis=-1, keepdims=True)
    zc = z - mu
    var = jnp.mean(zc * zc, axis=-1, keepdims=True)
    xn = zc * lax.rsqrt(var + EPS) * v[1:2] + v[2:3]
    return xn, xn * (1.0 + v[4:5]) + v[3:4]


def _route(logits):
    lane = lax.broadcasted_iota(jnp.int32, logits.shape, 1)
    big = jnp.int32(1 << 20)
    is_group = (lane >= N_EXPERTS) & (lane < N_EXPERTS + N_GROUPS)
    gl = jnp.where(is_group, logits, NEG_INF)
    gmax = jnp.max(gl, axis=1, keepdims=True)
    g_idx = jnp.min(jnp.where(gl == gmax, lane, big), axis=1, keepdims=True) - N_EXPERTS
    g_w = 1.0 / jnp.sum(jnp.exp(gl - gmax), axis=1, keepdims=True)
    lo = g_idx * EXPERTS_PER_GROUP
    el = jnp.where((lane >= lo) & (lane < lo + EXPERTS_PER_GROUP), logits, NEG_INF)
    v1 = jnp.max(el, axis=1, keepdims=True)
    i1 = jnp.min(jnp.where(el == v1, lane, big), axis=1, keepdims=True)
    el2 = jnp.where(lane == i1, NEG_INF, el)
    v2 = jnp.max(el2, axis=1, keepdims=True)
    i2 = jnp.min(jnp.where(el2 == v2, lane, big), axis=1, keepdims=True)
    e2 = jnp.exp(v2 - v1)
    den = 1.0 + e2
    w1 = (1.0 / den) * g_w
    w2 = (e2 / den) * g_w
    comb = jnp.where(lane == i1, w1, 0.0) + jnp.where(lane == i2, w2, 0.0)
    comb = jnp.where(lane == N_EXPERTS, i1.astype(F32), comb)
    return jnp.where(lane == N_EXPERTS + 1, i2.astype(F32), comb)


def _ln_router_kernel(x_ref, y_ref, v_ref, wr_ref, br_ref, xo_ref, hx_ref):
    d = x_ref.shape[1]
    xn, h = _deepnorm(x_ref[...], y_ref[...], v_ref[0])
    xo_ref[...] = xn
    logits = jnp.dot(h.astype(BF16), wr_ref[...].astype(BF16), preferred_element_type=F32) + br_ref[...]
    hx_ref[:, :d] = h
    hx_ref[:, d:] = _route(logits)


def ln_router(x, y, vecs, w_r, b_r, n_lat, rows):
    d = x.shape[1]
    tr = _pick(math.gcd(rows, n_lat), (256, 128, 64, 32, 16, 8))
    row, vec = _row_specs(tr, d, n_lat // tr)
    return pl.pallas_call(
        _ln_router_kernel, grid=(rows // tr,),
        in_specs=[row, row, vec, pl.BlockSpec((d, LANES), lambda i: (0, 0)),
                  pl.BlockSpec((1, LANES), lambda i: (0, 0))],
        out_specs=[row, pl.BlockSpec((tr, d + LANES), lambda i: (i, 0))],
        out_shape=[jax.ShapeDtypeStruct((rows, d), F32), jax.ShapeDtypeStruct((rows, d + LANES), F32)],
        compiler_params=_params("parallel"), name="ln_router",
    )(x, y, vecs, w_r, b_r)


def _row_specs(tr, d, n_lat_tiles):
    row = pl.BlockSpec((tr, d), lambda i: (i, 0))
    vec = pl.BlockSpec((1, 8, d), lambda i: ((i >= n_lat_tiles).astype(jnp.int32), 0, 0))
    return row, vec


def modulate(x, vecs, n_lat):
    t, d = x.shape
    tr = _pick(math.gcd(t, n_lat), (256, 128, 64, 32, 16, 8))
    row, vec = _row_specs(tr, d, n_lat // tr)
    return pl.pallas_call(
        _modulate_kernel, grid=(t // tr,), in_specs=[row, vec], out_specs=row,
        out_shape=jax.ShapeDtypeStruct((t, d), BF16),
        compiler_params=_params("parallel"), name="modulate",
    )(x, vecs)


def _row_copy(src_hbm, row, dst_vmem, dst_row, sem):
    return pltpu.make_async_copy(src_hbm.at[pl.ds(row, 1)], dst_vmem.at[pl.ds(dst_row, 1)], sem)


def _combine_ln_kernel(pos_ref, x_ref, y_hbm, v_ref, xo_ref, *rest, tr, n_tiles):
    h_refs, (ybuf, sems) = rest[:-2], rest[-2:]
    i = pl.program_id(0)
    slot = i % 2

    def issue(tile, to_slot):
        def body(r, carry):
            t = tile * tr + r
            _row_copy(y_hbm, pos_ref[2 * t], ybuf.at[to_slot], r, sems.at[to_slot]).start()
            _row_copy(y_hbm, pos_ref[2 * t + 1], ybuf.at[to_slot], tr + r, sems.at[to_slot]).start()
            return carry
        lax.fori_loop(0, tr, body, 0, unroll=8)

    @pl.when(i == 0)
    def _():
        issue(0, 0)

    @pl.when(i + 1 < n_tiles)
    def _():
        issue(i + 1, 1 - slot)

    pltpu.make_async_copy(y_hbm.at[pl.ds(0, 2 * tr)], ybuf.at[slot], sems.at[slot]).wait()
    y = ybuf[slot, :tr] + ybuf[slot, tr:]
    xn, h = _deepnorm(x_ref[...], y, v_ref[0])
    xo_ref[...] = xn
    if h_refs:
        h_refs[0][...] = h.astype(h_refs[0].dtype)


def combine_ln(x, y_sorted, pair_pos, vecs, n_lat, rows, emit_h=True):
    d = x.shape[1]
    tr = _pick(math.gcd(rows, n_lat), (128, 64, 32, 16, 8))
    n_tiles = rows // tr
    n_lat_tiles = n_lat // tr
    row = pl.BlockSpec((tr, d), lambda i, pos: (i, 0))
    vec = pl.BlockSpec((1, 8, d), lambda i, pos: ((i >= n_lat_tiles).astype(jnp.int32), 0, 0))
    out_shape = [jax.ShapeDtypeStruct((rows, d), F32)]
    out_specs = [row]
    if emit_h:
        out_shape.append(jax.ShapeDtypeStruct((rows, d), BF16))
        out_specs.append(row)
    return pl.pallas_call(
        functools.partial(_combine_ln_kernel, tr=tr, n_tiles=n_tiles),
        grid_spec=pltpu.PrefetchScalarGridSpec(
            num_scalar_prefetch=1, grid=(n_tiles,),
            in_specs=[row, pl.BlockSpec(memory_space=pl.ANY), vec],
            out_specs=out_specs,
            scratch_shapes=[pltpu.VMEM((2, 2 * tr, d), F32), pltpu.SemaphoreType.DMA((2,))]),
        out_shape=out_shape, compiler_params=_params("arbitrary"), name="combine_ln",
    )(pair_pos, x, y_sorted, vecs)


def _mm_kernel(*refs, n_extra, prologue, epilogue, transpose_out):
    a_ref, w_ref = refs[0], refs[1]
    extras = refs[2:2 + n_extra]
    o_ref, wb_ref = refs[2 + n_extra], refs[3 + n_extra]

    @pl.when(pl.program_id(1) == 0)
    def _():
        wb_ref[...] = w_ref[...].astype(BF16)

    a = a_ref[...]
    if prologue is not None:
        a = prologue(a, *extras)
    acc = jnp.dot(a.astype(BF16), wb_ref[...], preferred_element_type=F32)
    if epilogue is not None:
        acc = epilogue(acc, *extras)
    if transpose_out:
        acc = acc.T
    o_ref[...] = acc.astype(o_ref.dtype)


def matmul(a, w, *, rows, col0, n_cols, out_dtype, prologue=None, epilogue=None, extras=(),
           extra_specs=(), out_mult=1, transpose_out=False, name="matmul"):
    k = a.shape[1]
    tm = _pick(rows, (1024, 768, 512, 256, 128))
    tn = _pick(math.gcd(n_cols, col0) if col0 else n_cols, (512, 256, 128))
    cb0 = col0 // tn
    kern = functools.partial(_mm_kernel, n_extra=len(extras), prologue=prologue, epilogue=epilogue,
                             transpose_out=transpose_out)
    if transpose_out:
        out_spec = pl.BlockSpec((tn * out_mult, tm), lambda j, i: (j, i))
        out_shape = jax.ShapeDtypeStruct((n_cols * out_mult, rows), out_dtype)
    else:
        out_spec = pl.BlockSpec((tm, tn * out_mult), lambda j, i: (i, j))
        out_shape = jax.ShapeDtypeStruct((rows, n_cols * out_mult), out_dtype)
    return pl.pallas_call(
        kern,
        grid=(n_cols // tn, rows // tm),
        in_specs=[pl.BlockSpec((tm, k), lambda j, i: (i, 0)),
                  pl.BlockSpec((k, tn), lambda j, i: (0, cb0 + j)),
                  *extra_specs],
        out_specs=out_spec,
        out_shape=out_shape,
        scratch_shapes=[pltpu.VMEM((k, tn), BF16)],
        compiler_params=_params("parallel", "arbitrary"),
        name=name,
    )(a, w, *extras)


def _swap_pairs(x, half):
    n = x.shape[-1]
    lane = lax.broadcasted_iota(jnp.int32, x.shape, x.ndim - 1)
    fwd = pltpu.roll(x, n - half, axis=x.ndim - 1)
    bwd = pltpu.roll(x, half, axis=x.ndim - 1)
    return jnp.where((lane % (2 * half)) < half, fwd, bwd)


def _rope(x, cos, sin_signed, half):
    return x * cos + _swap_pairs(x, half) * sin_signed


def _rms(x, gain):
    return x * lax.rsqrt(jnp.mean(x * x, axis=-1, keepdims=True) + EPS) * gain


def _head_epilogue(acc, cos_ref, sin_ref, gain_ref, *, half, scale, norm, rope_chunks, split_halves):
    outs = []
    for c in range(acc.shape[1] // LANES):
        x = acc[:, c * LANES:(c + 1) * LANES]
        if norm:
            ssq = jnp.dot((x * x).astype(BF16), jnp.ones((LANES, LANES), BF16), preferred_element_type=F32)
            x = x * lax.rsqrt(ssq * (1.0 / LANES) + EPS) * gain_ref[...]
        if rope_chunks is None or (c % rope_chunks[1]) == rope_chunks[0]:
            x = _rope(x, cos_ref[...], sin_ref[...], half)
        if scale != 1.0:
            x = x * scale
        if split_halves:
            lane = lax.broadcasted_iota(jnp.int32, x.shape, 1)
            outs.append(jnp.where(lane < LANES // 2, x, 0.0))
            outs.append(jnp.where(lane >= LANES // 2, x, 0.0))
        else:
            outs.append(x)
    return jnp.concatenate(outs, axis=1) if len(outs) > 1 else outs[0]


def project_heads(a, w, tables, gain, *, rows, col0, n_cols, half, scale, norm=False,
                  rope_chunks=None, split_halves=False, name="proj"):
    cos, sin = tables
    tm = _pick(rows, (1024, 768, 512, 256, 128))
    if gain is None:
        gain = jnp.ones((1, LANES), F32)
    epi = functools.partial(_head_epilogue, half=half, scale=scale, norm=norm, rope_chunks=rope_chunks,
                            split_halves=split_halves)
    return matmul(
        a, w, rows=rows, col0=col0, n_cols=n_cols, out_dtype=BF16, epilogue=epi,
        extras=(cos, sin, gain),
        extra_specs=(pl.BlockSpec((tm, LANES), lambda j, i: (i, 0)),
                     pl.BlockSpec((tm, LANES), lambda j, i: (i, 0)),
                     pl.BlockSpec((1, LANES), lambda j, i: (0, 0))),
        out_mult=2 if split_halves else 1, name=name)


def rms_matmul(a, gain, w, *, rows, n_cols, out_dtype, epilogue=None, extras=(), extra_specs=(),
               transpose_out=False, name="rms_mm"):
    k = a.shape[1]
    pro = lambda a_t, g_ref, *rest: _rms(a_t, g_ref[...])
    epi = None if epilogue is None else (lambda acc, g_ref, *rest: epilogue(acc, *rest))
    return matmul(a, w, rows=rows, col0=0, n_cols=n_cols, out_dtype=out_dtype, prologue=pro, epilogue=epi,
                  extras=(gain.reshape(1, k), *extras),
                  extra_specs=(pl.BlockSpec((1, k), lambda j, i: (0, 0)), *extra_specs),
                  transpose_out=transpose_out, name=name)


def _kq(k, q):
    return lax.dot_general(k, q, (((1,), (1,)), ((), ())), preferred_element_type=F32)


def _col_reduce(x, op):
    rows, n = x.shape
    parts = 8 if rows % 64 == 0 else 1
    if parts > 1:
        x = op(x.reshape(parts, rows // parts, n), axis=1)
    return op(x, axis=0, keepdims=True)


def _flash_kernel(*refs, n_kv, G, dq, has_k2, diff, nq_lat, nk, kc, qcols):
    refs = list(refs)
    if diff is not None:
        lam_ref, sub_ref = refs.pop(0), refs.pop(0)
    q_ref, k_ref = refs.pop(0), refs.pop(0)
    k2_ref = refs.pop(0) if has_k2 else None
    vt_ref, kx_ref = refs.pop(0), refs.pop(0)
    kx2_ref = refs.pop(0) if has_k2 else None
    vxt_ref = refs.pop(0)
    o_ref, m_scr, acc_scr = refs
    i, j = pl.program_id(1), pl.program_id(2)
    tq = q_ref.shape[0]

    def folded_q(s):
        parts = [q_ref[:, (s * G + g) * dq:(s * G + g + 1) * dq] for g in range(G)]
        return jnp.concatenate(parts, axis=0) if G > 1 else parts[0]

    def scores(qs, k_r, k2_r, s, rows, cols):
        k = k_r[rows, s * LANES:(s + 1) * LANES]
        if k2_r is not None:
            k = jnp.concatenate([k, k2_r[rows, :]], axis=1)
        return _kq(k, qs[s][cols])

    def update(s, cols, st, vt):
        m_prev = m_scr[s, :, cols]
        m_new = jnp.maximum(m_prev, _col_reduce(st, jnp.max))
        p = jnp.exp2((st - m_new).astype(BF16))
        alpha = jnp.exp2(m_prev - m_new)
        vt1 = jnp.concatenate([vt, jnp.ones((ONES_ROWS, vt.shape[1]), BF16)], axis=0)
        acc_scr[s, :, cols] = acc_scr[s, :, cols] * alpha + jnp.dot(vt1, p, preferred_element_type=F32)
        m_scr[s, :, cols] = m_new

    def run(k_r, k2_r, vt_r, row_slices):
        qs = [folded_q(s) for s in range(n_kv)]
        col_slices = [slice(c0, c0 + qcols) for c0 in range(0, G * tq, qcols)]
        items = [(s, rows, cols) for rows in row_slices for s in range(n_kv) for cols in col_slices]
        st_next = scores(qs, k_r, k2_r, *items[0])
        for t, (s, rows, cols) in enumerate(items):
            st = st_next
            if t + 1 < len(items):
                st_next = scores(qs, k_r, k2_r, *items[t + 1])
            update(s, cols, st, vt_r[s * LANES:(s + 1) * LANES, rows])

    @pl.when(j == 0)
    def _():
        m_scr[...] = jnp.full(m_scr.shape, NEG_INF, F32)
        acc_scr[...] = jnp.zeros(acc_scr.shape, F32)
        run(kx_ref, kx2_ref, vxt_ref, [slice(None)])

    @pl.when(i < nq_lat)
    def _():
        run(k_ref, k2_ref, vt_ref, [slice(cc * kc, (cc + 1) * kc) for cc in range(k_ref.shape[0] // kc)])

    def normalized(s):
        return acc_scr[s, :LANES] / acc_scr[s, LANES:LANES + 1]

    @pl.when(j == nk - 1)
    def _():
        if diff is None:
            for s in range(n_kv):
                o = normalized(s)
                for g in range(G):
                    c = s * G + g
                    o_ref[:, c * LANES:(c + 1) * LANES] = o[:, g * tq:(g + 1) * tq].T.astype(o_ref.dtype)
        else:
            lv = lam_ref[...]
            lam = (jnp.exp(jnp.sum(lv[0:1] * lv[1:2], axis=1, keepdims=True))
                   - jnp.exp(jnp.sum(lv[2:3] * lv[3:4], axis=1, keepdims=True)) + diff)
            for s in range(n_kv):
                o = normalized(s)
                o = o[:, :tq] - lam * o[:, tq:]
                o = _rms(o.T, sub_ref[...]) * (1.0 - diff)
                o_ref[:, s * LANES:(s + 1) * LANES] = o.astype(o_ref.dtype)


def flash_attention(q, k, vt, *, tq, n_lat, n_ctx, with_ctx_q, name, **kw):
    kw.update(n_lat=n_lat, n_ctx=n_ctx)
    if tq == n_ctx or not with_ctx_q:
        return _flash_call(q, k, vt, tq=tq, qb0=0, nq_lat=n_lat // tq, n_ctx_q=1 if with_ctx_q else 0,
                           name=name, **kw)
    o_lat = _flash_call(q, k, vt, tq=tq, qb0=0, nq_lat=n_lat // tq, n_ctx_q=0, name=name, **kw)
    o_ctx = _flash_call(q, k, vt, tq=n_ctx, qb0=n_lat // n_ctx, nq_lat=0, n_ctx_q=1, name=name + "_ctx", **kw)
    return jnp.concatenate([o_lat, o_ctx], axis=0)


def _flash_call(q, k, vt, *, k2=None, diff=None, n_groups, n_kv, G, dq, n_lat, n_ctx, tq, qb0, nq_lat,
                n_ctx_q, name):
    nqb = nq_lat + n_ctx_q
    tk = _pick(n_lat, (ITEMS_PER_STEP * KEY_CHUNK // n_kv, 1024, 512, 256, 128))
    nk = n_lat // tk if nq_lat else 1
    kc = min(tk, KEY_CHUNK)
    cb = n_lat // n_ctx
    kw = n_kv * LANES
    n_slots = n_kv * G
    ow = (n_kv if diff is not None else n_slots) * LANES

    def jmap(i, j):
        return jnp.where(i < nq_lat, j, 0)

    args, specs = [], []
    if diff is not None:
        lam_vecs, subln, lambda_init = diff
        args += [lam_vecs, subln.reshape(1, LANES)]
        specs += [pl.BlockSpec(lam_vecs.shape, lambda h, i, j: (0, 0)),
                  pl.BlockSpec((1, LANES), lambda h, i, j: (0, 0))]
    args += [q, k]
    specs += [pl.BlockSpec((tq, n_slots * dq), lambda h, i, j: (qb0 + i, h)),
              pl.BlockSpec((tk, kw), lambda h, i, j: (jmap(i, j), h))]
    if k2 is not None:
        args.append(k2)
        specs.append(pl.BlockSpec((tk, LANES), lambda h, i, j: (jmap(i, j), 0)))
    args += [vt, k]
    specs += [pl.BlockSpec((kw, tk), lambda h, i, j: (h, jmap(i, j))),
              pl.BlockSpec((n_ctx, kw), lambda h, i, j: (cb, h))]
    if k2 is not None:
        args.append(k2)
        specs.append(pl.BlockSpec((n_ctx, LANES), lambda h, i, j: (cb, 0)))
    args.append(vt)
    specs.append(pl.BlockSpec((kw, n_ctx), lambda h, i, j: (h, cb)))
    kern = functools.partial(_flash_kernel, n_kv=n_kv, G=G, dq=dq, has_k2=k2 is not None,
                             diff=None if diff is None else diff[2],
                             nq_lat=nq_lat, nk=nk, kc=kc, qcols=min(G * tq, QUERY_COLS))
    return pl.pallas_call(
        kern,
        grid=(n_groups, nqb, nk),
        in_specs=specs,
        out_specs=pl.BlockSpec((tq, ow), lambda h, i, j: (i, h)),
        out_shape=jax.ShapeDtypeStruct((nqb * tq, n_groups * ow), BF16),
        scratch_shapes=[pltpu.VMEM((n_kv, 1, G * tq), F32),
                        pltpu.VMEM((n_kv, LANES + ONES_ROWS, G * tq), F32)],
        compiler_params=_params("parallel", "parallel", "arbitrary"),
        name=name,
    )(*args)


def _window_kernel(sink_ref, q_ref, kp_ref, kc_ref, kn_ref, kx_ref, vp_ref, vc_ref, vn_ref, vx_ref,
                   o_ref, *, G, tq, nq):
    h, i = pl.program_id(0), pl.program_id(1)
    q = jnp.concatenate([q_ref[:, g * HEAD_DIM:(g + 1) * HEAD_DIM] for g in range(G)], axis=0)
    k = jnp.concatenate([kp_ref[...], kc_ref[...], kn_ref[...], kx_ref[...]], axis=0)
    vt = jnp.concatenate([vp_ref[...], vc_ref[...], vn_ref[...], vx_ref[...]], axis=1)
    n_keys, n_q = k.shape[0], G * tq
    n_band = tq + 2 * WINDOW
    r = lax.broadcasted_iota(jnp.int32, (n_keys, tq), 0)
    qi = lax.broadcasted_iota(jnp.int32, (n_keys, tq), 1)
    rel = r - WINDOW - qi
    kpos = i * tq - WINDOW + r
    band = (jnp.abs(rel) <= WINDOW) & (kpos >= 0) & (kpos < nq * tq) & (i * tq + qi < nq * tq)
    bias = jnp.where(band | (r >= n_band), 0.0, NEG_INF)
    st = _kq(k, q) + jnp.concatenate([bias] * G, axis=1)
    head = lax.broadcasted_iota(jnp.int32, (1, n_q), 1) // tq
    sink = jnp.zeros((1, n_q), F32)
    for g in range(G):
        sink = jnp.where(head == g, sink_ref[h * G + g] * LOG2E, sink)
    m = jnp.maximum(_col_reduce(st, jnp.max), sink)
    p = jnp.exp2((st - m).astype(BF16))
    vt1 = jnp.concatenate([vt, jnp.ones((ONES_ROWS, n_keys), BF16)], axis=0)
    acc = jnp.dot(vt1, p, preferred_element_type=F32)
    o = acc[:LANES] / (acc[LANES:LANES + 1] + jnp.exp2(sink - m))
    for g in range(G):
        o_ref[:, g * HEAD_DIM:(g + 1) * HEAD_DIM] = o[:, g * tq:(g + 1) * tq].T.astype(o_ref.dtype)


def window_attention(q, k, vt, sink, *, n_heads_kv, G, n_lat, n_ctx, with_ctx_q):
    tq = 2 * WINDOW
    assert n_ctx == tq
    nq = n_lat // tq
    nqb = nq + (1 if with_ctx_q else 0)
    nb = n_lat // WINDOW
    cb = n_lat // n_ctx
    w = WINDOW

    def prev_map(h, i):
        return (jnp.maximum(2 * i - 1, 0), h)

    def next_map(h, i):
        return (jnp.minimum(2 * i + 2, nb - 1), h)

    k_specs = [pl.BlockSpec((w, HEAD_DIM), prev_map),
               pl.BlockSpec((tq, HEAD_DIM), lambda h, i: (i, h)),
               pl.BlockSpec((w, HEAD_DIM), next_map),
               pl.BlockSpec((n_ctx, HEAD_DIM), lambda h, i: (cb, h))]
    vt_specs = [pl.BlockSpec((HEAD_DIM, w), lambda h, i: prev_map(h, i)[::-1]),
                pl.BlockSpec((HEAD_DIM, tq), lambda h, i: (h, i)),
                pl.BlockSpec((HEAD_DIM, w), lambda h, i: next_map(h, i)[::-1]),
                pl.BlockSpec((HEAD_DIM, n_ctx), lambda h, i: (h, cb))]
    kern = functools.partial(_window_kernel, G=G, tq=tq, nq=nq)
    return pl.pallas_call(
        kern,
        grid=(n_heads_kv, nqb),
        in_specs=[pl.BlockSpec(memory_space=pltpu.SMEM),
                  pl.BlockSpec((tq, G * HEAD_DIM), lambda h, i: (i, h)),
                  *k_specs, *vt_specs],
        out_specs=pl.BlockSpec((tq, G * HEAD_DIM), lambda h, i: (i, h)),
        out_shape=jax.ShapeDtypeStruct((nqb * tq, n_heads_kv * G * HEAD_DIM), BF16),
        compiler_params=_params("parallel", "parallel"),
        name="window_attention",
    )(sink, q, k, k, k, k, vt, vt, vt, vt)


def moe_plan(hx, d, tm):
    rows = hx.shape[0]
    n_pairs = 2 * rows
    n_tiles = n_pairs // tm + N_EXPERTS
    e_flat = hx[:, d + N_EXPERTS:d + N_EXPERTS + 2].astype(jnp.int32).reshape(n_pairs)
    order = jnp.argsort(e_flat, stable=True).astype(jnp.int32)
    rank = jnp.argsort(order).astype(jnp.int32)
    counts = jnp.sum(e_flat[:, None] == jnp.arange(N_EXPERTS, dtype=jnp.int32)[None, :], axis=0, dtype=jnp.int32)
    tiles_e = (counts + tm - 1) // tm
    tile_end = jnp.cumsum(tiles_e)
    tile_start = tile_end - tiles_e
    seg_start = jnp.cumsum(counts) - counts
    pair_slot = tile_start[e_flat] * tm + rank - seg_start[e_flat]
    n_used = tile_end[-1]
    tile_ids = jnp.minimum(jnp.arange(n_tiles, dtype=jnp.int32), n_used - 1)
    tile_expert = jnp.sum(tile_ids[:, None] >= tile_end[None, :], axis=1, dtype=jnp.int32)
    e_slot = jnp.repeat(tile_expert, tm)
    r_slot = jnp.arange(n_tiles * tm, dtype=jnp.int32) - tile_start[e_slot] * tm
    valid = (r_slot < counts[e_slot]) & (jnp.arange(n_tiles * tm, dtype=jnp.int32) < n_used * tm)
    src_pair = order[jnp.clip(seg_start[e_slot] + r_slot, 0, n_pairs - 1)]
    src_tok = jnp.where(valid, src_pair // 2, 0)
    return src_tok, tile_expert, n_used.reshape(1), pair_slot


def _moe_experts_kernel(te_ref, src_ref, n_ref, hx_hbm, wgu_ref, wd_ref, o_ref, xbuf, sems, wgu_bf, wd_bf,
                        *, tm, d):
    i = pl.program_id(0)
    n_used = n_ref[0]
    slot = i % 2

    def issue(tile, to_slot):
        def body(r, carry):
            _row_copy(hx_hbm, src_ref[tile * tm + r], xbuf.at[to_slot], r, sems.at[to_slot]).start()
            return carry
        lax.fori_loop(0, tm, body, 0, unroll=8)

    @pl.when(i == 0)
    def _():
        issue(0, 0)

    @pl.when(i + 1 < n_used)
    def _():
        issue(i + 1, 1 - slot)

    @pl.when(i < n_used)
    def _():
        e = te_ref[i]

        @pl.when((i == 0) | (e != te_ref[jnp.maximum(i - 1, 0)]))
        def _():
            wgu_bf[...] = wgu_ref[0, 0].astype(BF16)
            wd_bf[...] = wd_ref[0, 0].astype(BF16)

        pltpu.make_async_copy(hx_hbm.at[pl.ds(0, tm)], xbuf.at[slot], sems.at[slot]).wait()
        xg = xbuf[slot]
        gu = jnp.dot(xg[:, :d].astype(BF16), wgu_bf[...], preferred_element_type=F32)
        f = gu.shape[1] // 2
        gate, up = gu[:, :f], gu[:, f:]
        comb = xg[:, d:]
        lane = lax.broadcasted_iota(jnp.int32, comb.shape, 1)
        c = jnp.sum(jnp.where(lane == e, comb, 0.0), axis=1, keepdims=True)
        act = (gate * jax.nn.sigmoid(gate)) * up * c
        o_ref[...] = jnp.dot(act.astype(BF16), wd_bf[...], preferred_element_type=F32)

    @pl.when(i >= n_used)
    def _():
        o_ref[...] = jnp.zeros(o_ref.shape, o_ref.dtype)


def moe_experts(hx, plan, w_gate_up, w_down, layer, tm):
    src_tok, tile_expert, n_used, _ = plan
    d = hx.shape[1] - LANES
    f2 = w_gate_up.shape[3]
    n_tiles = tile_expert.shape[0]
    return pl.pallas_call(
        functools.partial(_moe_experts_kernel, tm=tm, d=d),
        grid_spec=pltpu.PrefetchScalarGridSpec(
            num_scalar_prefetch=3, grid=(n_tiles,),
            in_specs=[pl.BlockSpec(memory_space=pl.ANY),
                      pl.BlockSpec((1, 1, d, f2), lambda i, te, src, n: (layer, te[i], 0, 0)),
                      pl.BlockSpec((1, 1, f2 // 2, d), lambda i, te, src, n: (layer, te[i], 0, 0))],
            out_specs=pl.BlockSpec((tm, d), lambda i, te, src, n: (i, 0)),
            scratch_shapes=[pltpu.VMEM((2, tm, d + LANES), F32), pltpu.SemaphoreType.DMA((2,)),
                            pltpu.VMEM((d, f2), BF16), pltpu.VMEM((f2 // 2, d), BF16)]),
        out_shape=jax.ShapeDtypeStruct((n_tiles * tm, d), F32),
        compiler_params=_params("arbitrary"), name="moe_experts",
    )(tile_expert, src_tok, n_used, hx, w_gate_up, w_down)


def _rope_tables(n_lat, n_ctx, rot_dim):
    rows = n_lat // GRID_W
    r, col = jnp.meshgrid(jnp.arange(rows, dtype=F32), jnp.arange(GRID_W, dtype=F32), indexing="ij")
    pos = jnp.stack([r.reshape(-1), col.reshape(-1)], axis=-1)
    n_freq = rot_dim // 4
    inv_freq = ROPE_BASE ** (-jnp.arange(n_freq, dtype=F32) / n_freq)
    ang = pos[:, :, None] * inv_freq
    cos, sin = jnp.cos(ang), jnp.sin(ang)
    cos_full = jnp.concatenate([cos[:, 0], cos[:, 0], cos[:, 1], cos[:, 1]], axis=-1)
    sin_full = jnp.concatenate([-sin[:, 0], sin[:, 0], -sin[:, 1], sin[:, 1]], axis=-1)
    reps = LANES // rot_dim
    cos_full = jnp.tile(cos_full, (1, reps))
    sin_full = jnp.tile(sin_full, (1, reps))
    cos_full = jnp.concatenate([cos_full, jnp.ones((n_ctx, LANES), F32)], axis=0)
    sin_full = jnp.concatenate([sin_full, jnp.zeros((n_ctx, LANES), F32)], axis=0)
    return cos_full, sin_full


def _vec_pack(mods, gate_idx, ln_g, ln_b, next_mods, shift_idx, scale_idx):
    d = mods.shape[-1]
    z = jnp.zeros((2, d), F32)
    gate = mods[:2, gate_idx] if gate_idx is not None else z
    g = jnp.broadcast_to(ln_g, (2, d)) if ln_g is not None else z
    b = jnp.broadcast_to(ln_b, (2, d)) if ln_b is not None else z
    shift = next_mods[:2, shift_idx] if next_mods is not None else z
    scale = next_mods[:2, scale_idx] if next_mods is not None else z
    return jnp.stack([gate, g, b, shift, scale, z, z, z], axis=1)


def kernel(x, c, ctx, c_ctx, ada_w, ada_b, ln_g, ln_b, win_w_qkv, win_w_o, win_sink, qkn_w_qkv, qkn_q_gain, qkn_k_gain, qkn_w_o, mla_w_a, mla_q_gain, mla_kv_gain, mla_w_qb, mla_w_kvb, mla_w_o, diff_w_qkv, diff_lambda, diff_subln, diff_w_o, moe_w_group, moe_b_group, moe_w_expert, moe_b_expert, moe_w_gate_up, moe_w_down):
    b, n_lat, d = x.shape
    n_ctx = ctx.shape[1]
    assert b == 1
    t = n_lat + n_ctx
    n_heads = d // HEAD_DIM
    n_kv = n_heads // 4
    grp = n_heads // n_kv

    tab_h = _rope_tables(n_lat, n_ctx, HEAD_DIM)
    tab_r = _rope_tables(n_lat, n_ctx, C_ROPE)

    cond8 = jnp.zeros((8, d), F32).at[0].set(c[0]).at[1].set(c_ctx)
    mods = modulation_all(cond8, ada_w, ada_b).reshape(DEPTH, 8, N_MOD, d)

    xs = jnp.concatenate([x[0], ctx[0]], axis=0)
    h = modulate(xs, _vec_pack(mods[0], None, None, None, mods[0], 0, 1), n_lat)

    for i in range(DEPTH):
        kind = i % 4
        need_ctx = i < DEPTH - 1
        rows = t if need_ctx else n_lat
        att_kw = dict(n_lat=n_lat, n_ctx=n_ctx, with_ctx_q=need_ctx)
        if kind == 0:
            w = win_w_qkv[0]
            scale = HEAD_DIM ** -0.5 * LOG2E
            q = project_heads(h, w, tab_h, None, rows=rows, col0=0, n_cols=d, half=32, scale=scale, name="win_q")
            k = project_heads(h, w, tab_h, None, rows=t, col0=d, n_cols=n_kv * HEAD_DIM, half=32, scale=1.0,
                              name="win_k")
            vt = matmul(h, w, rows=t, col0=d + n_kv * HEAD_DIM, n_cols=n_kv * HEAD_DIM, out_dtype=BF16,
                        transpose_out=True, name="win_vt")
            o = window_attention(q, k, vt, win_sink[0], n_heads_kv=n_kv, G=grp, **att_kw)
            w_o = win_w_o[0]
        elif kind == 1:
            w = qkn_w_qkv[0]
            scale = HEAD_DIM ** -0.5 * LOG2E
            q = project_heads(h, w, tab_h, qkn_q_gain[0].reshape(1, HEAD_DIM), rows=rows, col0=0, n_cols=d, half=32,
                              scale=scale, norm=True, name="qkn_q")
            k = project_heads(h, w, tab_h, qkn_k_gain[0].reshape(1, HEAD_DIM), rows=t, col0=d,
                              n_cols=n_kv * HEAD_DIM, half=32, scale=1.0, norm=True, name="qkn_k")
            vt = matmul(h, w, rows=t, col0=d + n_kv * HEAD_DIM, n_cols=n_kv * HEAD_DIM, out_dtype=BF16,
                        transpose_out=True, name="qkn_vt")
            o = flash_attention(q, k, vt, tq=n_ctx, n_groups=n_kv, n_kv=1, G=grp, dq=HEAD_DIM, name="qkn_attn",
                                **att_kw)
            w_o = qkn_w_o[0]
        elif kind == 2:
            w_a = mla_w_a[0]
            q_rank = mla_q_gain.shape[1]
            kv_rank = mla_kv_gain.shape[1]
            scale = (C_NOPE + C_ROPE) ** -0.5 * LOG2E
            cq = matmul(h, w_a, rows=rows, col0=0, n_cols=q_rank, out_dtype=F32, name="mla_cq")
            ckv = matmul(h, w_a, rows=t, col0=q_rank, n_cols=kv_rank, out_dtype=F32, name="mla_ckv")
            w_pe = jnp.pad(w_a[:, q_rank + kv_rank:], ((0, 0), (0, LANES - C_ROPE)))
            k_pe = project_heads(h, w_pe, tab_r, None, rows=t, col0=0, n_cols=LANES, half=16, scale=1.0,
                                 name="mla_kpe")
            w_qb = mla_w_qb[0].reshape(q_rank, n_heads, C_NOPE + C_ROPE)
            w_qb = jnp.pad(w_qb, ((0, 0), (0, 0), (0, 2 * LANES - C_NOPE - C_ROPE))).reshape(q_rank, n_heads * 2 * LANES)
            w_kvb = mla_w_kvb[0].reshape(kv_rank, n_heads, C_NOPE + C_V)
            w_kn = w_kvb[:, :, :C_NOPE].reshape(kv_rank, n_heads * C_NOPE)
            w_v = w_kvb[:, :, C_NOPE:].reshape(kv_rank, n_heads * C_V)
            tm = _pick(rows, (1024, 768, 512, 256, 128))
            q_epi = functools.partial(_head_epilogue, half=16, scale=scale, norm=False, rope_chunks=(1, 2),
                                      split_halves=False)
            q = rms_matmul(cq, mla_q_gain[0], w_qb, rows=rows, n_cols=n_heads * 2 * LANES, out_dtype=BF16,
                           epilogue=lambda acc, cs, sn: q_epi(acc, cs, sn, None),
                           extras=tab_r,
                           extra_specs=(pl.BlockSpec((tm, LANES), lambda j, i: (i, 0)),
                                        pl.BlockSpec((tm, LANES), lambda j, i: (i, 0))), name="mla_q")
            k_nope = rms_matmul(ckv, mla_kv_gain[0], w_kn, rows=t, n_cols=n_heads * C_NOPE, out_dtype=BF16,
                                name="mla_kn")
            vt = rms_matmul(ckv, mla_kv_gain[0], w_v, rows=t, n_cols=n_heads * C_V, out_dtype=BF16,
                            transpose_out=True, name="mla_vt")
            hp = 2
            o = flash_attention(q, k_nope, vt, tq=_pick(n_lat, (QUERY_COLS, n_ctx)), k2=k_pe,
                                n_groups=n_heads // hp, n_kv=hp, G=1, dq=2 * LANES, name="mla_attn", **att_kw)
            w_o = mla_w_o[0]
        else:
            w = diff_w_qkv[0]
            lambda_init = 0.8 - 0.6 * math.exp(-0.3 * i)
            scale = DF_HEAD ** -0.5 * LOG2E
            n_dh = d // (2 * DF_HEAD)
            q = project_heads(h, w, tab_r, None, rows=rows, col0=0, n_cols=d, half=16, scale=scale,
                              split_halves=True, name="diff_q")
            k = project_heads(h, w, tab_r, None, rows=t, col0=d, n_cols=d, half=16, scale=1.0, name="diff_k")
            vt = matmul(h, w, rows=t, col0=2 * d, n_cols=d, out_dtype=BF16, transpose_out=True, name="diff_vt")
            hp = 2
            o = flash_attention(q, k, vt, tq=_pick(n_lat, (QUERY_COLS // 2, n_ctx)),
                                diff=(diff_lambda[0], diff_subln[0], lambda_init), n_groups=n_dh // hp,
                                n_kv=hp, G=2, dq=LANES, name="diff_attn", **att_kw)
            w_o = diff_w_o[0]

        a = matmul(o, w_o, rows=rows, col0=0, n_cols=d, out_dtype=BF16, name="attn_out")
        w_r = jnp.concatenate([moe_w_expert[i], moe_w_group[i],
                               jnp.zeros((d, LANES - N_EXPERTS - N_GROUPS), F32)], axis=1)
        b_r = jnp.concatenate([moe_b_expert[i], moe_b_group[i],
                               jnp.zeros((LANES - N_EXPERTS - N_GROUPS,), F32)]).reshape(1, LANES)
        xs, hx = ln_router(xs, a, _vec_pack(mods[i], 2, ln_g[i, 0], ln_b[i, 0], mods[i], 3, 4), w_r, b_r,
                           n_lat, rows)
        plan = moe_plan(hx, d, MOE_TILE)
        y_slots = moe_experts(hx, plan, moe_w_gate_up, moe_w_down, i, MOE_TILE)
        if i + 1 < DEPTH:
            xs, h = combine_ln(xs, y_slots, plan[3], _vec_pack(mods[i], 5, ln_g[i, 1], ln_b[i, 1], mods[i + 1], 0, 1),
                               n_lat, rows)
        else:
            (xs,) = combine_ln(xs, y_slots, plan[3], _vec_pack(mods[i], 5, ln_g[i, 1], ln_b[i, 1], None, 0, 1),
                               n_lat, rows, emit_h=False)
    return xs[:n_lat].reshape(b, n_lat, d)
```

```python
import functools
import math

import jax
import jax.numpy as jnp
from jax import lax
from jax.experimental import pallas as pl
from jax.experimental.pallas import tpu as pltpu

F32 = jnp.float32
BF16 = jnp.bfloat16

DEPTH = 4
GRID_W = 64
HEAD_DIM = 128
ROPE_BASE = 10000.0
EPS = 1e-6
NEG_INF = -1e30
N_MOD = 6
WINDOW = 128
C_NOPE = 128
C_ROPE = 64
C_V = 128
DF_HEAD = 64
N_GROUPS = 4
EXPERTS_PER_GROUP = 6
N_EXPERTS = N_GROUPS * EXPERTS_PER_GROUP
ALPHA = (2.0 * DEPTH) ** 0.25
LOG2E = math.log2(math.e)
LANES = 128
SCORE_LOOKAHEAD = 1
ITEMS_PER_STEP = 8
KEY_CHUNK = 512
MOE_TILE = 256
QUERY_COLS = 1024
ONES_ROWS = 16
VMEM_LIMIT_BYTES = 56 * 1024 * 1024


def _pick(n, cands):
    for c in cands:
        if n % c == 0:
            return c
    raise ValueError(f"no tile in {cands} divides {n}")


def _params(*sem):
    return pltpu.CompilerParams(dimension_semantics=sem, vmem_limit_bytes=VMEM_LIMIT_BYTES)


def _mod_kernel(cond_ref, w_ref, b_ref, o_ref):
    a = cond_ref[...]
    a = (a * jax.nn.sigmoid(a)).astype(BF16)
    o_ref[0] = jnp.dot(a, w_ref[0].astype(BF16), preferred_element_type=F32) + b_ref[0]


def modulation_all(cond8, ada_w, ada_b):
    depth, d, n = ada_w.shape
    tn = _pick(n, (512, 256, 128))
    return pl.pallas_call(
        _mod_kernel,
        grid=(depth, n // tn),
        in_specs=[pl.BlockSpec((8, d), lambda l, j: (0, 0)),
                  pl.BlockSpec((1, d, tn), lambda l, j: (l, 0, j)),
                  pl.BlockSpec((1, 1, tn), lambda l, j: (l, 0, j))],
        out_specs=pl.BlockSpec((1, 8, tn), lambda l, j: (l, 0, j)),
        out_shape=jax.ShapeDtypeStruct((depth, 8, n), F32),
        compiler_params=_params("parallel", "parallel"),
        name="modulation",
    )(cond8, ada_w, ada_b.reshape(depth, 1, n))


def _modulate_kernel(x_ref, v_ref, h_ref):
    v = v_ref[0]
    h_ref[...] = (x_ref[...] * (1.0 + v[4:5]) + v[3:4]).astype(h_ref.dtype)


def _deepnorm(x, y, v):
    z = ALPHA * x + v[0:1] * y
    mu = jnp.mean(z, axis=-1, keepdims=True)
    zc = z - mu
    var = jnp.mean(zc * zc, axis=-1, keepdims=True)
    xn = zc * lax.rsqrt(var + EPS) * v[1:2] + v[2:3]
    return xn, xn * (1.0 + v[4:5]) + v[3:4]


def _route(logits):
    lane = lax.broadcasted_iota(jnp.int32, logits.shape, 1)
    big = jnp.int32(1 << 20)
    is_group = (lane >= N_EXPERTS) & (lane < N_EXPERTS + N_GROUPS)
    gl = jnp.where(is_group, logits, NEG_INF)
    gmax = jnp.max(gl, axis=1, keepdims=True)
    g_idx = jnp.min(jnp.where(gl == gmax, lane, big), axis=1, keepdims=True) - N_EXPERTS
    g_w = 1.0 / jnp.sum(jnp.exp(gl - gmax), axis=1, keepdims=True)
    lo = g_idx * EXPERTS_PER_GROUP
    el = jnp.where((lane >= lo) & (lane < lo + EXPERTS_PER_GROUP), logits, NEG_INF)
    v1 = jnp.max(el, axis=1, keepdims=True)
    i1 = jnp.min(jnp.where(el == v1, lane, big), axis=1, keepdims=True)
    el2 = jnp.where(lane == i1, NEG_INF, el)
    v2 = jnp.max(el2, axis=1, keepdims=True)
    i2 = jnp.min(jnp.where(el2 == v2, lane, big), axis=1, keepdims=True)
    e2 = jnp.exp(v2 - v1)
    den = 1.0 + e2
    w1 = (1.0 / den) * g_w
    w2 = (e2 / den) * g_w
    comb = jnp.where(lane == i1, w1, 0.0) + jnp.where(lane == i2, w2, 0.0)
    comb = jnp.where(lane == N_EXPERTS, i1.astype(F32), comb)
    return jnp.where(lane == N_EXPERTS + 1, i2.astype(F32), comb)


def _ln_router_kernel(x_ref, y_ref, v_ref, wr_ref, br_ref, xo_ref, hx_ref):
    d = x_ref.shape[1]
    xn, h = _deepnorm(x_ref[...], y_ref[...], v_ref[0])
    xo_ref[...] = xn
    logits = jnp.dot(h.astype(BF16), wr_ref[...].astype(BF16), preferred_element_type=F32) + br_ref[...]
    hx_ref[:, :d] = h
    hx_ref[:, d:] = _route(logits)


def ln_router(x, y, vecs, w_r, b_r, n_lat, rows):
    d = x.shape[1]
    tr = _pick(math.gcd(rows, n_lat), (256, 128, 64, 32, 16, 8))
    row, vec = _row_specs(tr, d, n_lat // tr)
    return pl.pallas_call(
        _ln_router_kernel, grid=(rows // tr,),
        in_specs=[row, row, vec, pl.BlockSpec((d, LANES), lambda i: (0, 0)),
                  pl.BlockSpec((1, LANES), lambda i: (0, 0))],
        out_specs=[row, pl.BlockSpec((tr, d + LANES), lambda i: (i, 0))],
        out_shape=[jax.ShapeDtypeStruct((rows, d), F32), jax.ShapeDtypeStruct((rows, d + LANES), F32)],
        compiler_params=_params("parallel"), name="ln_router",
    )(x, y, vecs, w_r, b_r)


def _row_specs(tr, d, n_lat_tiles):
    row = pl.BlockSpec((tr, d), lambda i: (i, 0))
    vec = pl.BlockSpec((1, 8, d), lambda i: ((i >= n_lat_tiles).astype(jnp.int32), 0, 0))
    return row, vec


def modulate(x, vecs, n_lat):
    t, d = x.shape
    tr = _pick(math.gcd(t, n_lat), (256, 128, 64, 32, 16, 8))
    row, vec = _row_specs(tr, d, n_lat // tr)
    return pl.pallas_call(
        _modulate_kernel, grid=(t // tr,), in_specs=[row, vec], out_specs=row,
        out_shape=jax.ShapeDtypeStruct((t, d), BF16),
        compiler_params=_params("parallel"), name="modulate",
    )(x, vecs)


def _row_copy(src_hbm, row, dst_vmem, dst_row, sem):
    return pltpu.make_async_copy(src_hbm.at[pl.ds(row, 1)], dst_vmem.at[pl.ds(dst_row, 1)], sem)


def _combine_ln_kernel(pos_ref, x_ref, y_hbm, v_ref, xo_ref, *rest, tr, n_tiles):
    h_refs, (ybuf, sems) = rest[:-2], rest[-2:]
    i = pl.program_id(0)
    slot = i % 2

    def issue(tile, to_slot):
        def body(r, carry):
            t = tile * tr + r
            _row_copy(y_hbm, pos_ref[2 * t], ybuf.at[to_slot], r, sems.at[to_slot]).start()
            _row_copy(y_hbm, pos_ref[2 * t + 1], ybuf.at[to_slot], tr + r, sems.at[to_slot]).start()
            return carry
        lax.fori_loop(0, tr, body, 0, unroll=8)

    @pl.when(i == 0)
    def _():
        issue(0, 0)

    @pl.when(i + 1 < n_tiles)
    def _():
        issue(i + 1, 1 - slot)

    pltpu.make_async_copy(y_hbm.at[pl.ds(0, 2 * tr)], ybuf.at[slot], sems.at[slot]).wait()
    y = ybuf[slot, :tr] + ybuf[slot, tr:]
    xn, h = _deepnorm(x_ref[...], y, v_ref[0])
    xo_ref[...] = xn
    if h_refs:
        h_refs[0][...] = h.astype(h_refs[0].dtype)


def combine_ln(x, y_sorted, pair_pos, vecs, n_lat, rows, emit_h=True):
    d = x.shape[1]
    tr = _pick(math.gcd(rows, n_lat), (128, 64, 32, 16, 8))
    n_tiles = rows // tr
    n_lat_tiles = n_lat // tr
    row = pl.BlockSpec((tr, d), lambda i, pos: (i, 0))
    vec = pl.BlockSpec((1, 8, d), lambda i, pos: ((i >= n_lat_tiles).astype(jnp.int32), 0, 0))
    out_shape = [jax.ShapeDtypeStruct((rows, d), F32)]
    out_specs = [row]
    if emit_h:
        out_shape.append(jax.ShapeDtypeStruct((rows, d), BF16))
        out_specs.append(row)
    return pl.pallas_call(
        functools.partial(_combine_ln_kernel, tr=tr, n_tiles=n_tiles),
        grid_spec=pltpu.PrefetchScalarGridSpec(
            num_scalar_prefetch=1, grid=(n_tiles,),
            in_specs=[row, pl.BlockSpec(memory_space=pl.ANY), vec],
            out_specs=out_specs,
            scratch_shapes=[pltpu.VMEM((2, 2 * tr, d), F32), pltpu.SemaphoreType.DMA((2,))]),
        out_shape=out_shape, compiler_params=_params("arbitrary"), name="combine_ln",
    )(pair_pos, x, y_sorted, vecs)


def _mm_kernel(*refs, n_extra, epilogue, transpose_out):
    a_ref, w_ref = refs[0], refs[1]
    extras = refs[2:2 + n_extra]
    o_ref, wb_ref = refs[2 + n_extra], refs[3 + n_extra]

    @pl.when(pl.program_id(1) == 0)
    def _():
        wb_ref[...] = w_ref[...].astype(BF16)

    tm = a_ref.shape[0]
    if epilogue is None or transpose_out:
        acc = jnp.dot(a_ref[...], wb_ref[...], preferred_element_type=F32)
        if transpose_out:
            acc = acc.T
        o_ref[...] = acc.astype(o_ref.dtype)
        return
    half = tm // 2
    for r0 in (0, half):
        rows = pl.ds(r0, half)
        acc = jnp.dot(a_ref[rows, :], wb_ref[...], preferred_element_type=F32)
        acc = epilogue(acc, *[e.at[rows] if e.shape[0] == tm else e for e in extras])
        o_ref[rows, :] = acc.astype(o_ref.dtype)


def matmul(a, w, *, rows, col0, n_cols, out_dtype, epilogue=None, extras=(),
           extra_specs=(), out_mult=1, transpose_out=False, name="matmul"):
    k = a.shape[1]
    tm = _pick(rows, (1024, 768, 512, 256, 128))
    tn = _pick(math.gcd(n_cols, col0) if col0 else n_cols, (512, 256, 128))
    cb0 = col0 // tn
    kern = functools.partial(_mm_kernel, n_extra=len(extras), epilogue=epilogue, transpose_out=transpose_out)
    if transpose_out:
        out_spec = pl.BlockSpec((tn * out_mult, tm), lambda j, i: (j, i))
        out_shape = jax.ShapeDtypeStruct((n_cols * out_mult, rows), out_dtype)
    else:
        out_spec = pl.BlockSpec((tm, tn * out_mult), lambda j, i: (i, j))
        out_shape = jax.ShapeDtypeStruct((rows, n_cols * out_mult), out_dtype)
    return pl.pallas_call(
        kern,
        grid=(n_cols // tn, rows // tm),
        in_specs=[pl.BlockSpec((tm, k), lambda j, i: (i, 0)),
                  pl.BlockSpec((k, tn), lambda j, i: (0, cb0 + j)),
                  *extra_specs],
        out_specs=out_spec,
        out_shape=out_shape,
        scratch_shapes=[pltpu.VMEM((k, tn), BF16)],
        compiler_params=_params("parallel", "arbitrary"),
        name=name,
    )(a, w, *extras)


def _swap_pairs(x, half):
    n = x.shape[-1]
    lane = lax.broadcasted_iota(jnp.int32, x.shape, x.ndim - 1)
    fwd = pltpu.roll(x, n - half, axis=x.ndim - 1)
    bwd = pltpu.roll(x, half, axis=x.ndim - 1)
    return jnp.where((lane % (2 * half)) < half, fwd, bwd)


def _rope(x, cos, sin_signed, half):
    return x * cos + _swap_pairs(x, half) * sin_signed


def _rms(x, gain):
    return x * lax.rsqrt(jnp.mean(x * x, axis=-1, keepdims=True) + EPS) * gain


def _head_epilogue(acc, cos_ref, sin_ref, gain_ref, *, half, scale, norm, rope_chunks, split_halves):
    outs = []
    for c in range(acc.shape[1] // LANES):
        x = acc[:, c * LANES:(c + 1) * LANES]
        if norm:
            ssq = jnp.dot((x * x).astype(BF16), jnp.ones((LANES, LANES), BF16), preferred_element_type=F32)
            x = x * lax.rsqrt(ssq * (1.0 / LANES) + EPS) * gain_ref[...]
        if rope_chunks is None or (c % rope_chunks[1]) == rope_chunks[0]:
            x = _rope(x, cos_ref[...], sin_ref[...], half)
        if scale != 1.0:
            x = x * scale
        if split_halves:
            lane = lax.broadcasted_iota(jnp.int32, x.shape, 1)
            outs.append(jnp.where(lane < LANES // 2, x, 0.0))
            outs.append(jnp.where(lane >= LANES // 2, x, 0.0))
        else:
            outs.append(x)
    return jnp.concatenate(outs, axis=1) if len(outs) > 1 else outs[0]


def project_heads(a, w, tables, gain, *, rows, col0, n_cols, half, scale, norm=False,
                  rope_chunks=None, split_halves=False, name="proj"):
    cos, sin = tables
    tm = _pick(rows, (1024, 768, 512, 256, 128))
    if gain is None:
        gain = jnp.ones((1, LANES), F32)
    epi = functools.partial(_head_epilogue, half=half, scale=scale, norm=norm, rope_chunks=rope_chunks,
                            split_halves=split_halves)
    return matmul(
        a, w, rows=rows, col0=col0, n_cols=n_cols, out_dtype=BF16, epilogue=epi,
        extras=(cos, sin, gain),
        extra_specs=(pl.BlockSpec((tm, LANES), lambda j, i: (i, 0)),
                     pl.BlockSpec((tm, LANES), lambda j, i: (i, 0)),
                     pl.BlockSpec((1, LANES), lambda j, i: (0, 0))),
        out_mult=2 if split_halves else 1, name=name)


def _rms_mm_kernel(*refs, n_extra, epilogue, transpose_out):
    a_ref, g_ref, w_ref = refs[:3]
    extras = refs[3:3 + n_extra]
    o_ref, an_ref = refs[3 + n_extra:]

    @pl.when(pl.program_id(1) == 0)
    def _():
        an_ref[...] = _rms(a_ref[...], g_ref[...]).astype(BF16)

    acc = jnp.dot(an_ref[...], w_ref[...].astype(BF16), preferred_element_type=F32)
    if epilogue is not None:
        acc = epilogue(acc, *extras)
    if transpose_out:
        acc = acc.T
    o_ref[...] = acc.astype(o_ref.dtype)


def rms_matmul(a, gain, w, *, rows, n_cols, out_dtype, epilogue=None, extras=(), extra_specs=(),
               transpose_out=False, name="rms_mm"):
    k = a.shape[1]
    tm = _pick(rows, (1024, 768, 512, 256, 128))
    tn = _pick(n_cols, (512, 256, 128))
    kern = functools.partial(_rms_mm_kernel, n_extra=len(extras), epilogue=epilogue, transpose_out=transpose_out)
    if transpose_out:
        out_spec = pl.BlockSpec((tn, tm), lambda i, j: (j, i))
        out_shape = jax.ShapeDtypeStruct((n_cols, rows), out_dtype)
    else:
        out_spec = pl.BlockSpec((tm, tn), lambda i, j: (i, j))
        out_shape = jax.ShapeDtypeStruct((rows, n_cols), out_dtype)
    return pl.pallas_call(
        kern,
        grid=(rows // tm, n_cols // tn),
        in_specs=[pl.BlockSpec((tm, k), lambda i, j: (i, 0)),
                  pl.BlockSpec((1, k), lambda i, j: (0, 0)),
                  pl.BlockSpec((k, tn), lambda i, j: (0, j)),
                  *extra_specs],
        out_specs=out_spec,
        out_shape=out_shape,
        scratch_shapes=[pltpu.VMEM((tm, k), BF16)],
        compiler_params=_params("parallel", "arbitrary"),
        name=name,
    )(a, gain.reshape(1, k), w, *extras)


def _kq(k, q):
    return lax.dot_general(k, q, (((1,), (1,)), ((), ())), preferred_element_type=F32)


def _col_reduce(x, op):
    rows, n = x.shape
    parts = 8 if rows % 64 == 0 else 1
    if parts > 1:
        x = op(x.reshape(parts, rows // parts, n), axis=1)
    return op(x, axis=0, keepdims=True)


def _flash_kernel(*refs, n_kv, G, dq, has_k2, diff, nq_lat, nk, kc, qcols):
    refs = list(refs)
    if diff is not None:
        lam_ref, sub_ref = refs.pop(0), refs.pop(0)
    q_ref, k_ref = refs.pop(0), refs.pop(0)
    k2_ref = refs.pop(0) if has_k2 else None
    vt_ref, kx_ref = refs.pop(0), refs.pop(0)
    kx2_ref = refs.pop(0) if has_k2 else None
    vxt_ref = refs.pop(0)
    o_ref, m_scr, acc_scr = refs
    i, j = pl.program_id(1), pl.program_id(2)
    tq = q_ref.shape[0]

    def folded_q(s):
        parts = [q_ref[:, (s * G + g) * dq:(s * G + g + 1) * dq] for g in range(G)]
        return jnp.concatenate(parts, axis=0) if G > 1 else parts[0]

    def scores(qs, k_r, k2_r, s, rows, cols):
        k = k_r[rows, s * LANES:(s + 1) * LANES]
        if k2_r is not None:
            k = jnp.concatenate([k, k2_r[rows, :]], axis=1)
        return _kq(k, qs[s][cols])

    def update(s, cols, st, vt):
        m_prev = m_scr[s, :, cols]
        m_new = jnp.maximum(m_prev, _col_reduce(st, jnp.max))
        p = jnp.exp2((st - m_new).astype(BF16))
        alpha = jnp.exp2(m_prev - m_new)
        vt1 = jnp.concatenate([vt, jnp.ones((ONES_ROWS, vt.shape[1]), BF16)], axis=0)
        acc_scr[s, :, cols] = acc_scr[s, :, cols] * alpha + jnp.dot(vt1, p, preferred_element_type=F32)
        m_scr[s, :, cols] = m_new

    def run(k_r, k2_r, vt_r, row_slices):
        qs = [folded_q(s) for s in range(n_kv)]
        col_slices = [slice(c0, c0 + qcols) for c0 in range(0, G * tq, qcols)]
        items = [(s, rows, cols) for rows in row_slices for s in range(n_kv) for cols in col_slices]
        ahead = [scores(qs, k_r, k2_r, *it) for it in items[:SCORE_LOOKAHEAD]]
        for t, (s, rows, cols) in enumerate(items):
            st = ahead.pop(0)
            if t + SCORE_LOOKAHEAD < len(items):
                ahead.append(scores(qs, k_r, k2_r, *items[t + SCORE_LOOKAHEAD]))
            update(s, cols, st, vt_r[s * LANES:(s + 1) * LANES, rows])

    @pl.when(j == 0)
    def _():
        m_scr[...] = jnp.full(m_scr.shape, NEG_INF, F32)
        acc_scr[...] = jnp.zeros(acc_scr.shape, F32)
        run(kx_ref, kx2_ref, vxt_ref, [slice(None)])

    @pl.when(i < nq_lat)
    def _():
        run(k_ref, k2_ref, vt_ref, [slice(cc * kc, (cc + 1) * kc) for cc in range(k_ref.shape[0] // kc)])

    def normalized(s):
        return acc_scr[s, :LANES] / acc_scr[s, LANES:LANES + 1]

    @pl.when(j == nk - 1)
    def _():
        if diff is None:
            for s in range(n_kv):
                o = normalized(s)
                for g in range(G):
                    c = s * G + g
                    o_ref[:, c * LANES:(c + 1) * LANES] = o[:, g * tq:(g + 1) * tq].T.astype(o_ref.dtype)
        else:
            lv = lam_ref[...]
            lam = (jnp.exp(jnp.sum(lv[0:1] * lv[1:2], axis=1, keepdims=True))
                   - jnp.exp(jnp.sum(lv[2:3] * lv[3:4], axis=1, keepdims=True)) + diff)
            for s in range(n_kv):
                o = normalized(s)
                o = o[:, :tq] - lam * o[:, tq:]
                o = _rms(o.T, sub_ref[...]) * (1.0 - diff)
                o_ref[:, s * LANES:(s + 1) * LANES] = o.astype(o_ref.dtype)


def flash_attention(q, k, vt, *, tq, n_lat, n_ctx, with_ctx_q, name, **kw):
    kw.update(n_lat=n_lat, n_ctx=n_ctx)
    if tq == n_ctx or not with_ctx_q:
        return _flash_call(q, k, vt, tq=tq, qb0=0, nq_lat=n_lat // tq, n_ctx_q=1 if with_ctx_q else 0,
                           name=name, **kw)
    o_lat = _flash_call(q, k, vt, tq=tq, qb0=0, nq_lat=n_lat // tq, n_ctx_q=0, name=name, **kw)
    o_ctx = _flash_call(q, k, vt, tq=n_ctx, qb0=n_lat // n_ctx, nq_lat=0, n_ctx_q=1, name=name + "_ctx", **kw)
    return jnp.concatenate([o_lat, o_ctx], axis=0)


def _flash_call(q, k, vt, *, k2=None, diff=None, n_groups, n_kv, G, dq, n_lat, n_ctx, tq, qb0, nq_lat,
                n_ctx_q, name):
    nqb = nq_lat + n_ctx_q
    tk = _pick(n_lat, (ITEMS_PER_STEP * KEY_CHUNK // n_kv, 1024, 512, 256, 128))
    nk = n_lat // tk if nq_lat else 1
    kc = min(tk, KEY_CHUNK)
    cb = n_lat // n_ctx
    kw = n_kv * LANES
    n_slots = n_kv * G
    ow = (n_kv if diff is not None else n_slots) * LANES

    def jmap(i, j):
        return jnp.where(i < nq_lat, j, 0)

    args, specs = [], []
    if diff is not None:
        lam_vecs, subln, lambda_init = diff
        args += [lam_vecs, subln.reshape(1, LANES)]
        specs += [pl.BlockSpec(lam_vecs.shape, lambda h, i, j: (0, 0)),
                  pl.BlockSpec((1, LANES), lambda h, i, j: (0, 0))]
    args += [q, k]
    specs += [pl.BlockSpec((tq, n_slots * dq), lambda h, i, j: (qb0 + i, h)),
              pl.BlockSpec((tk, kw), lambda h, i, j: (jmap(i, j), h))]
    if k2 is not None:
        args.append(k2)
        specs.append(pl.BlockSpec((tk, LANES), lambda h, i, j: (jmap(i, j), 0)))
    args += [vt, k]
    specs += [pl.BlockSpec((kw, tk), lambda h, i, j: (h, jmap(i, j))),
              pl.BlockSpec((n_ctx, kw), lambda h, i, j: (cb, h))]
    if k2 is not None:
        args.append(k2)
        specs.append(pl.BlockSpec((n_ctx, LANES), lambda h, i, j: (cb, 0)))
    args.append(vt)
    specs.append(pl.BlockSpec((kw, n_ctx), lambda h, i, j: (h, cb)))
    kern = functools.partial(_flash_kernel, n_kv=n_kv, G=G, dq=dq, has_k2=k2 is not None,
                             diff=None if diff is None else diff[2],
                             nq_lat=nq_lat, nk=nk, kc=kc, qcols=min(G * tq, QUERY_COLS))
    return pl.pallas_call(
        kern,
        grid=(n_groups, nqb, nk),
        in_specs=specs,
        out_specs=pl.BlockSpec((tq, ow), lambda h, i, j: (i, h)),
        out_shape=jax.ShapeDtypeStruct((nqb * tq, n_groups * ow), BF16),
        scratch_shapes=[pltpu.VMEM((n_kv, 1, G * tq), F32),
                        pltpu.VMEM((n_kv, LANES + ONES_ROWS, G * tq), F32)],
        compiler_params=_params("parallel", "parallel", "arbitrary"),
        name=name,
    )(*args)


def _window_kernel(sink_ref, q_ref, kp_ref, kc_ref, kn_ref, kx_ref, vp_ref, vc_ref, vn_ref, vx_ref,
                   o_ref, *, G, tq, nq):
    h, i = pl.program_id(0), pl.program_id(1)
    q = jnp.concatenate([q_ref[:, g * HEAD_DIM:(g + 1) * HEAD_DIM] for g in range(G)], axis=0)
    k = jnp.concatenate([kp_ref[...], kc_ref[...], kn_ref[...], kx_ref[...]], axis=0)
    vt = jnp.concatenate([vp_ref[...], vc_ref[...], vn_ref[...], vx_ref[...]], axis=1)
    n_keys, n_q = k.shape[0], G * tq
    n_band = tq + 2 * WINDOW
    r = lax.broadcasted_iota(jnp.int32, (n_keys, tq), 0)
    qi = lax.broadcasted_iota(jnp.int32, (n_keys, tq), 1)
    rel = r - WINDOW - qi
    kpos = i * tq - WINDOW + r
    band = (jnp.abs(rel) <= WINDOW) & (kpos >= 0) & (kpos < nq * tq) & (i * tq + qi < nq * tq)
    bias = jnp.where(band | (r >= n_band), 0.0, NEG_INF)
    st = _kq(k, q) + jnp.concatenate([bias] * G, axis=1)
    head = lax.broadcasted_iota(jnp.int32, (1, n_q), 1) // tq
    sink = jnp.zeros((1, n_q), F32)
    for g in range(G):
        sink = jnp.where(head == g, sink_ref[h * G + g] * LOG2E, sink)
    m = jnp.maximum(_col_reduce(st, jnp.max), sink)
    p = jnp.exp2((st - m).astype(BF16))
    vt1 = jnp.concatenate([vt, jnp.ones((ONES_ROWS, n_keys), BF16)], axis=0)
    acc = jnp.dot(vt1, p, preferred_element_type=F32)
    o = acc[:LANES] / (acc[LANES:LANES + 1] + jnp.exp2(sink - m))
    for g in range(G):
        o_ref[:, g * HEAD_DIM:(g + 1) * HEAD_DIM] = o[:, g * tq:(g + 1) * tq].T.astype(o_ref.dtype)


def window_attention(q, k, vt, sink, *, n_heads_kv, G, n_lat, n_ctx, with_ctx_q):
    tq = 2 * WINDOW
    assert n_ctx == tq
    nq = n_lat // tq
    nqb = nq + (1 if with_ctx_q else 0)
    nb = n_lat // WINDOW
    cb = n_lat // n_ctx
    w = WINDOW

    def prev_map(h, i):
        return (jnp.maximum(2 * i - 1, 0), h)

    def next_map(h, i):
        return (jnp.minimum(2 * i + 2, nb - 1), h)

    k_specs = [pl.BlockSpec((w, HEAD_DIM), prev_map),
               pl.BlockSpec((tq, HEAD_DIM), lambda h, i: (i, h)),
               pl.BlockSpec((w, HEAD_DIM), next_map),
               pl.BlockSpec((n_ctx, HEAD_DIM), lambda h, i: (cb, h))]
    vt_specs = [pl.BlockSpec((HEAD_DIM, w), lambda h, i: prev_map(h, i)[::-1]),
                pl.BlockSpec((HEAD_DIM, tq), lambda h, i: (h, i)),
                pl.BlockSpec((HEAD_DIM, w), lambda h, i: next_map(h, i)[::-1]),
                pl.BlockSpec((HEAD_DIM, n_ctx), lambda h, i: (h, cb))]
    kern = functools.partial(_window_kernel, G=G, tq=tq, nq=nq)
    return pl.pallas_call(
        kern,
        grid=(n_heads_kv, nqb),
        in_specs=[pl.BlockSpec(memory_space=pltpu.SMEM),
                  pl.BlockSpec((tq, G * HEAD_DIM), lambda h, i: (i, h)),
                  *k_specs, *vt_specs],
        out_specs=pl.BlockSpec((tq, G * HEAD_DIM), lambda h, i: (i, h)),
        out_shape=jax.ShapeDtypeStruct((nqb * tq, n_heads_kv * G * HEAD_DIM), BF16),
        compiler_params=_params("parallel", "parallel"),
        name="window_attention",
    )(sink, q, k, k, k, k, vt, vt, vt, vt)


def moe_plan(hx, d, tm):
    rows = hx.shape[0]
    n_pairs = 2 * rows
    n_tiles = n_pairs // tm + N_EXPERTS
    e_flat = hx[:, d + N_EXPERTS:d + N_EXPERTS + 2].astype(jnp.int32).reshape(n_pairs)
    order = jnp.argsort(e_flat, stable=True).astype(jnp.int32)
    rank = jnp.argsort(order).astype(jnp.int32)
    counts = jnp.sum(e_flat[:, None] == jnp.arange(N_EXPERTS, dtype=jnp.int32)[None, :], axis=0, dtype=jnp.int32)
    tiles_e = (counts + tm - 1) // tm
    tile_end = jnp.cumsum(tiles_e)
    tile_start = tile_end - tiles_e
    seg_start = jnp.cumsum(counts) - counts
    pair_slot = tile_start[e_flat] * tm + rank - seg_start[e_flat]
    n_used = tile_end[-1]
    tile_ids = jnp.minimum(jnp.arange(n_tiles, dtype=jnp.int32), n_used - 1)
    tile_expert = jnp.sum(tile_ids[:, None] >= tile_end[None, :], axis=1, dtype=jnp.int32)
    e_slot = jnp.repeat(tile_expert, tm)
    r_slot = jnp.arange(n_tiles * tm, dtype=jnp.int32) - tile_start[e_slot] * tm
    valid = (r_slot < counts[e_slot]) & (jnp.arange(n_tiles * tm, dtype=jnp.int32) < n_used * tm)
    src_pair = order[jnp.clip(seg_start[e_slot] + r_slot, 0, n_pairs - 1)]
    src_tok = jnp.where(valid, src_pair // 2, 0)
    return src_tok, tile_expert, n_used.reshape(1), pair_slot


def _moe_experts_kernel(te_ref, src_ref, n_ref, hx_hbm, wgu_ref, wd_ref, o_ref, xbuf, sems, wgu_bf, wd_bf,
                        *, tm, d):
    i = pl.program_id(0)
    n_used = n_ref[0]
    slot = i % 2

    def issue(tile, to_slot):
        def body(r, carry):
            _row_copy(hx_hbm, src_ref[tile * tm + r], xbuf.at[to_slot], r, sems.at[to_slot]).start()
            return carry
        lax.fori_loop(0, tm, body, 0, unroll=8)

    @pl.when(i == 0)
    def _():
        issue(0, 0)

    @pl.when(i + 1 < n_used)
    def _():
        issue(i + 1, 1 - slot)

    @pl.when(i < n_used)
    def _():
        e = te_ref[i]

        @pl.when((i == 0) | (e != te_ref[jnp.maximum(i - 1, 0)]))
        def _():
            wgu_bf[...] = wgu_ref[0, 0].astype(BF16)
            wd_bf[...] = wd_ref[0, 0].astype(BF16)

        pltpu.make_async_copy(hx_hbm.at[pl.ds(0, tm)], xbuf.at[slot], sems.at[slot]).wait()
        xg = xbuf[slot]
        gu = jnp.dot(xg[:, :d].astype(BF16), wgu_bf[...], preferred_element_type=F32)
        f = gu.shape[1] // 2
        gate, up = gu[:, :f], gu[:, f:]
        comb = xg[:, d:]
        lane = lax.broadcasted_iota(jnp.int32, comb.shape, 1)
        c = jnp.sum(jnp.where(lane == e, comb, 0.0), axis=1, keepdims=True)
        act = (gate * jax.nn.sigmoid(gate)) * up * c
        o_ref[...] = jnp.dot(act.astype(BF16), wd_bf[...], preferred_element_type=F32)

    @pl.when(i >= n_used)
    def _():
        o_ref[...] = jnp.zeros(o_ref.shape, o_ref.dtype)


def moe_experts(hx, plan, w_gate_up, w_down, layer, tm):
    src_tok, tile_expert, n_used, _ = plan
    d = hx.shape[1] - LANES
    f2 = w_gate_up.shape[3]
    n_tiles = tile_expert.shape[0]
    return pl.pallas_call(
        functools.partial(_moe_experts_kernel, tm=tm, d=d),
        grid_spec=pltpu.PrefetchScalarGridSpec(
            num_scalar_prefetch=3, grid=(n_tiles,),
            in_specs=[pl.BlockSpec(memory_space=pl.ANY),
                      pl.BlockSpec((1, 1, d, f2), lambda i, te, src, n: (layer, te[i], 0, 0)),
                      pl.BlockSpec((1, 1, f2 // 2, d), lambda i, te, src, n: (layer, te[i], 0, 0))],
            out_specs=pl.BlockSpec((tm, d), lambda i, te, src, n: (i, 0)),
            scratch_shapes=[pltpu.VMEM((2, tm, d + LANES), F32), pltpu.SemaphoreType.DMA((2,)),
                            pltpu.VMEM((d, f2), BF16), pltpu.VMEM((f2 // 2, d), BF16)]),
        out_shape=jax.ShapeDtypeStruct((n_tiles * tm, d), F32),
        compiler_params=_params("arbitrary"), name="moe_experts",
    )(tile_expert, src_tok, n_used, hx, w_gate_up, w_down)


def _rope_tables(n_lat, n_ctx, rot_dim):
    rows = n_lat // GRID_W
    r, col = jnp.meshgrid(jnp.arange(rows, dtype=F32), jnp.arange(GRID_W, dtype=F32), indexing="ij")
    pos = jnp.stack([r.reshape(-1), col.reshape(-1)], axis=-1)
    n_freq = rot_dim // 4
    inv_freq = ROPE_BASE ** (-jnp.arange(n_freq, dtype=F32) / n_freq)
    ang = pos[:, :, None] * inv_freq
    cos, sin = jnp.cos(ang), jnp.sin(ang)
    cos_full = jnp.concatenate([cos[:, 0], cos[:, 0], cos[:, 1], cos[:, 1]], axis=-1)
    sin_full = jnp.concatenate([-sin[:, 0], sin[:, 0], -sin[:, 1], sin[:, 1]], axis=-1)
    reps = LANES // rot_dim
    cos_full = jnp.tile(cos_full, (1, reps))
    sin_full = jnp.tile(sin_full, (1, reps))
    cos_full = jnp.concatenate([cos_full, jnp.ones((n_ctx, LANES), F32)], axis=0)
    sin_full = jnp.concatenate([sin_full, jnp.zeros((n_ctx, LANES), F32)], axis=0)
    return cos_full, sin_full


def _vec_pack(mods, gate_idx, ln_g, ln_b, next_mods, shift_idx, scale_idx):
    d = mods.shape[-1]
    z = jnp.zeros((2, d), F32)
    gate = mods[:2, gate_idx] if gate_idx is not None else z
    g = jnp.broadcast_to(ln_g, (2, d)) if ln_g is not None else z
    b = jnp.broadcast_to(ln_b, (2, d)) if ln_b is not None else z
    shift = next_mods[:2, shift_idx] if next_mods is not None else z
    scale = next_mods[:2, scale_idx] if next_mods is not None else z
    return jnp.stack([gate, g, b, shift, scale, z, z, z], axis=1)


def kernel(x, c, ctx, c_ctx, ada_w, ada_b, ln_g, ln_b, win_w_qkv, win_w_o, win_sink, qkn_w_qkv, qkn_q_gain, qkn_k_gain, qkn_w_o, mla_w_a, mla_q_gain, mla_kv_gain, mla_w_qb, mla_w_kvb, mla_w_o, diff_w_qkv, diff_lambda, diff_subln, diff_w_o, moe_w_group, moe_b_group, moe_w_expert, moe_b_expert, moe_w_gate_up, moe_w_down):
    b, n_lat, d = x.shape
    n_ctx = ctx.shape[1]
    assert b == 1
    t = n_lat + n_ctx
    n_heads = d // HEAD_DIM
    n_kv = n_heads // 4
    grp = n_heads // n_kv

    tab_h = _rope_tables(n_lat, n_ctx, HEAD_DIM)
    tab_r = _rope_tables(n_lat, n_ctx, C_ROPE)

    cond8 = jnp.zeros((8, d), F32).at[0].set(c[0]).at[1].set(c_ctx)
    mods = modulation_all(cond8, ada_w, ada_b).reshape(DEPTH, 8, N_MOD, d)

    xs = jnp.concatenate([x[0], ctx[0]], axis=0)
    h = modulate(xs, _vec_pack(mods[0], None, None, None, mods[0], 0, 1), n_lat)

    for i in range(DEPTH):
        kind = i % 4
        need_ctx = i < DEPTH - 1
        rows = t if need_ctx else n_lat
        att_kw = dict(n_lat=n_lat, n_ctx=n_ctx, with_ctx_q=need_ctx)
        if kind == 0:
            w = win_w_qkv[0]
            scale = HEAD_DIM ** -0.5 * LOG2E
            q = project_heads(h, w, tab_h, None, rows=rows, col0=0, n_cols=d, half=32, scale=scale, name="win_q")
            k = project_heads(h, w, tab_h, None, rows=t, col0=d, n_cols=n_kv * HEAD_DIM, half=32, scale=1.0,
                              name="win_k")
            vt = matmul(h, w, rows=t, col0=d + n_kv * HEAD_DIM, n_cols=n_kv * HEAD_DIM, out_dtype=BF16,
                        transpose_out=True, name="win_vt")
            o = window_attention(q, k, vt, win_sink[0], n_heads_kv=n_kv, G=grp, **att_kw)
            w_o = win_w_o[0]
        elif kind == 1:
            w = qkn_w_qkv[0]
            scale = HEAD_DIM ** -0.5 * LOG2E
            q = project_heads(h, w, tab_h, qkn_q_gain[0].reshape(1, HEAD_DIM), rows=rows, col0=0, n_cols=d, half=32,
                              scale=scale, norm=True, name="qkn_q")
            k = project_heads(h, w, tab_h, qkn_k_gain[0].reshape(1, HEAD_DIM), rows=t, col0=d,
                              n_cols=n_kv * HEAD_DIM, half=32, scale=1.0, norm=True, name="qkn_k")
            vt = matmul(h, w, rows=t, col0=d + n_kv * HEAD_DIM, n_cols=n_kv * HEAD_DIM, out_dtype=BF16,
                        transpose_out=True, name="qkn_vt")
            o = flash_attention(q, k, vt, tq=n_ctx, n_groups=n_kv, n_kv=1, G=grp, dq=HEAD_DIM, name="qkn_attn",
                                **att_kw)
            w_o = qkn_w_o[0]
        elif kind == 2:
            w_a = mla_w_a[0]
            q_rank = mla_q_gain.shape[1]
            kv_rank = mla_kv_gain.shape[1]
            scale = (C_NOPE + C_ROPE) ** -0.5 * LOG2E
            cq = matmul(h, w_a, rows=rows, col0=0, n_cols=q_rank, out_dtype=F32, name="mla_cq")
            ckv = matmul(h, w_a, rows=t, col0=q_rank, n_cols=kv_rank, out_dtype=F32, name="mla_ckv")
            w_pe = jnp.pad(w_a[:, q_rank + kv_rank:], ((0, 0), (0, LANES - C_ROPE)))
            k_pe = project_heads(h, w_pe, tab_r, None, rows=t, col0=0, n_cols=LANES, half=16, scale=1.0,
                                 name="mla_kpe")
            w_qb = mla_w_qb[0].reshape(q_rank, n_heads, C_NOPE + C_ROPE)
            w_qb = jnp.pad(w_qb, ((0, 0), (0, 0), (0, 2 * LANES - C_NOPE - C_ROPE))).reshape(q_rank, n_heads * 2 * LANES)
            w_kvb = mla_w_kvb[0].reshape(kv_rank, n_heads, C_NOPE + C_V)
            w_kn = w_kvb[:, :, :C_NOPE].reshape(kv_rank, n_heads * C_NOPE)
            w_v = w_kvb[:, :, C_NOPE:].reshape(kv_rank, n_heads * C_V)
            tm = _pick(rows, (1024, 768, 512, 256, 128))
            q_epi = functools.partial(_head_epilogue, half=16, scale=scale, norm=False, rope_chunks=(1, 2),
                                      split_halves=False)
            q = rms_matmul(cq, mla_q_gain[0], w_qb, rows=rows, n_cols=n_heads * 2 * LANES, out_dtype=BF16,
                           epilogue=lambda acc, cs, sn: q_epi(acc, cs, sn, None),
                           extras=tab_r,
                           extra_specs=(pl.BlockSpec((tm, LANES), lambda i, j: (i, 0)),
                                        pl.BlockSpec((tm, LANES), lambda i, j: (i, 0))), name="mla_q")
            k_nope = rms_matmul(ckv, mla_kv_gain[0], w_kn, rows=t, n_cols=n_heads * C_NOPE, out_dtype=BF16,
                                name="mla_kn")
            vt = rms_matmul(ckv, mla_kv_gain[0], w_v, rows=t, n_cols=n_heads * C_V, out_dtype=BF16,
                            transpose_out=True, name="mla_vt")
            hp = 2
            o = flash_attention(q, k_nope, vt, tq=_pick(n_lat, (QUERY_COLS, n_ctx)), k2=k_pe,
                                n_groups=n_heads // hp, n_kv=hp, G=1, dq=2 * LANES, name="mla_attn", **att_kw)
            w_o = mla_w_o[0]
        else:
            w = diff_w_qkv[0]
            lambda_init = 0.8 - 0.6 * math.exp(-0.3 * i)
            scale = DF_HEAD ** -0.5 * LOG2E
            n_dh = d // (2 * DF_HEAD)
            q = project_heads(h, w, tab_r, None, rows=rows, col0=0, n_cols=d, half=16, scale=scale,
                              split_halves=True, name="diff_q")
            k = project_heads(h, w, tab_r, None, rows=t, col0=d, n_cols=d, half=16, scale=1.0, name="diff_k")
            vt = matmul(h, w, rows=t, col0=2 * d, n_cols=d, out_dtype=BF16, transpose_out=True, name="diff_vt")
            hp = 2
            o = flash_attention(q, k, vt, tq=_pick(n_lat, (QUERY_COLS // 2, n_ctx)),
                                diff=(diff_lambda[0], diff_subln[0], lambda_init), n_groups=n_dh // hp,
                                n_kv=hp, G=2, dq=LANES, name="diff_attn", **att_kw)
            w_o = diff_w_o[0]

        a = matmul(o, w_o, rows=rows, col0=0, n_cols=d, out_dtype=BF16, name="attn_out")
        w_r = jnp.concatenate([moe_w_expert[i], moe_w_group[i],
                               jnp.zeros((d, LANES - N_EXPERTS - N_GROUPS), F32)], axis=1)
        b_r = jnp.concatenate([moe_b_expert[i], moe_b_group[i],
                               jnp.zeros((LANES - N_EXPERTS - N_GROUPS,), F32)]).reshape(1, LANES)
        xs, hx = ln_router(xs, a, _vec_pack(mods[i], 2, ln_g[i, 0], ln_b[i, 0], mods[i], 3, 4), w_r, b_r,
                           n_lat, rows)
        plan = moe_plan(hx, d, MOE_TILE)
        y_slots = moe_experts(hx, plan, moe_w_gate_up, moe_w_down, i, MOE_TILE)
        if i + 1 < DEPTH:
            xs, h = combine_ln(xs, y_slots, plan[3], _vec_pack(mods[i], 5, ln_g[i, 1], ln_b[i, 1], mods[i + 1], 0, 1),
                               n_lat, rows)
        else:
            (xs,) = combine_ln(xs, y_slots, plan[3], _vec_pack(mods[i], 5, ln_g[i, 1], ln_b[i, 1], None, 0, 1),
                               n_lat, rows, emit_h=False)
    return xs[:n_lat].reshape(b, n_lat, d)
```

```python
import functools
import math

import jax
import jax.numpy as jnp
from jax import lax
from jax.experimental import pallas as pl
from jax.experimental.pallas import tpu as pltpu

F32 = jnp.float32
BF16 = jnp.bfloat16

DEPTH = 4
GRID_W = 64
HEAD_DIM = 128
ROPE_BASE = 10000.0
EPS = 1e-6
NEG_INF = -1e30
N_MOD = 6
WINDOW = 128
C_NOPE = 128
C_ROPE = 64
C_V = 128
DF_HEAD = 64
N_GROUPS = 4
EXPERTS_PER_GROUP = 6
N_EXPERTS = N_GROUPS * EXPERTS_PER_GROUP
ALPHA = (2.0 * DEPTH) ** 0.25
LOG2E = math.log2(math.e)
LANES = 128
SCORE_LOOKAHEAD = 1
ITEMS_PER_STEP = 8
KEY_CHUNK = 512
MOE_TILE = 256
QUERY_COLS = 1024
ONES_ROWS = 16
VMEM_LIMIT_BYTES = 56 * 1024 * 1024


def _pick(n, cands):
    for c in cands:
        if n % c == 0:
            return c
    raise ValueError(f"no tile in {cands} divides {n}")


def _params(*sem):
    return pltpu.CompilerParams(dimension_semantics=sem, vmem_limit_bytes=VMEM_LIMIT_BYTES)


def _mod_kernel(cond_ref, w_ref, b_ref, o_ref):
    a = cond_ref[...]
    a = (a * jax.nn.sigmoid(a)).astype(BF16)
    o_ref[0] = jnp.dot(a, w_ref[0].astype(BF16), preferred_element_type=F32) + b_ref[0]


def modulation_all(cond8, ada_w, ada_b):
    depth, d, n = ada_w.shape
    tn = _pick(n, (512, 256, 128))
    return pl.pallas_call(
        _mod_kernel,
        grid=(depth, n // tn),
        in_specs=[pl.BlockSpec((8, d), lambda l, j: (0, 0)),
                  pl.BlockSpec((1, d, tn), lambda l, j: (l, 0, j)),
                  pl.BlockSpec((1, 1, tn), lambda l, j: (l, 0, j))],
        out_specs=pl.BlockSpec((1, 8, tn), lambda l, j: (l, 0, j)),
        out_shape=jax.ShapeDtypeStruct((depth, 8, n), F32),
        compiler_params=_params("parallel", "parallel"),
        name="modulation",
    )(cond8, ada_w, ada_b.reshape(depth, 1, n))


def _modulate_kernel(x_ref, v_ref, h_ref):
    v = v_ref[0]
    h_ref[...] = (x_ref[...] * (1.0 + v[4:5]) + v[3:4]).astype(h_ref.dtype)


def _deepnorm(x, y, v):
    z = ALPHA * x + v[0:1] * y
    mu = jnp.mean(z, axis=-1, keepdims=True)
    zc = z - mu
    var = jnp.mean(zc * zc, axis=-1, keepdims=True)
    xn = zc * lax.rsqrt(var + EPS) * v[1:2] + v[2:3]
    return xn, xn * (1.0 + v[4:5]) + v[3:4]


def _route(logits):
    lane = lax.broadcasted_iota(jnp.int32, logits.shape, 1)
    big = jnp.int32(1 << 20)
    is_group = (lane >= N_EXPERTS) & (lane < N_EXPERTS + N_GROUPS)
    gl = jnp.where(is_group, logits, NEG_INF)
    gmax = jnp.max(gl, axis=1, keepdims=True)
    g_idx = jnp.min(jnp.where(gl == gmax, lane, big), axis=1, keepdims=True) - N_EXPERTS
    g_w = 1.0 / jnp.sum(jnp.exp(gl - gmax), axis=1, keepdims=True)
    lo = g_idx * EXPERTS_PER_GROUP
    el = jnp.where((lane >= lo) & (lane < lo + EXPERTS_PER_GROUP), logits, NEG_INF)
    v1 = jnp.max(el, axis=1, keepdims=True)
    i1 = jnp.min(jnp.where(el == v1, lane, big), axis=1, keepdims=True)
    el2 = jnp.where(lane == i1, NEG_INF, el)
    v2 = jnp.max(el2, axis=1, keepdims=True)
    i2 = jnp.min(jnp.where(el2 == v2, lane, big), axis=1, keepdims=True)
    e2 = jnp.exp(v2 - v1)
    den = 1.0 + e2
    w1 = (1.0 / den) * g_w
    w2 = (e2 / den) * g_w
    comb = jnp.where(lane == i1, w1, 0.0) + jnp.where(lane == i2, w2, 0.0)
    comb = jnp.where(lane == N_EXPERTS, i1.astype(F32), comb)
    return jnp.where(lane == N_EXPERTS + 1, i2.astype(F32), comb)


def _ln_router_kernel(x_ref, y_ref, v_ref, wr_ref, br_ref, xo_ref, hx_ref):
    d = x_ref.shape[1]
    xn, h = _deepnorm(x_ref[...], y_ref[...], v_ref[0])
    xo_ref[...] = xn
    logits = jnp.dot(h.astype(BF16), wr_ref[...].astype(BF16), preferred_element_type=F32) + br_ref[...]
    hx_ref[:, :d] = h
    hx_ref[:, d:] = _route(logits)


def ln_router(x, y, vecs, w_r, b_r, n_lat, rows):
    d = x.shape[1]
    tr = _pick(math.gcd(rows, n_lat), (256, 128, 64, 32, 16, 8))
    row, vec = _row_specs(tr, d, n_lat // tr)
    return pl.pallas_call(
        _ln_router_kernel, grid=(rows // tr,),
        in_specs=[row, row, vec, pl.BlockSpec((d, LANES), lambda i: (0, 0)),
                  pl.BlockSpec((1, LANES), lambda i: (0, 0))],
        out_specs=[row, pl.BlockSpec((tr, d + LANES), lambda i: (i, 0))],
        out_shape=[jax.ShapeDtypeStruct((rows, d), F32), jax.ShapeDtypeStruct((rows, d + LANES), F32)],
        compiler_params=_params("parallel"), name="ln_router",
    )(x, y, vecs, w_r, b_r)


def _row_specs(tr, d, n_lat_tiles):
    row = pl.BlockSpec((tr, d), lambda i: (i, 0))
    vec = pl.BlockSpec((1, 8, d), lambda i: ((i >= n_lat_tiles).astype(jnp.int32), 0, 0))
    return row, vec


def modulate(x, vecs, n_lat):
    t, d = x.shape
    tr = _pick(math.gcd(t, n_lat), (256, 128, 64, 32, 16, 8))
    row, vec = _row_specs(tr, d, n_lat // tr)
    return pl.pallas_call(
        _modulate_kernel, grid=(t // tr,), in_specs=[row, vec], out_specs=row,
        out_shape=jax.ShapeDtypeStruct((t, d), BF16),
        compiler_params=_params("parallel"), name="modulate",
    )(x, vecs)


def _row_copy(src_hbm, row, dst_vmem, dst_row, sem):
    return pltpu.make_async_copy(src_hbm.at[pl.ds(row, 1)], dst_vmem.at[pl.ds(dst_row, 1)], sem)


def _combine_ln_kernel(pos_ref, x_ref, y_hbm, v_ref, xo_ref, *rest, tr, n_tiles):
    h_refs, (ybuf, sems) = rest[:-2], rest[-2:]
    i = pl.program_id(0)
    slot = i % 2

    def issue(tile, to_slot):
        def body(r, carry):
            t = tile * tr + r
            _row_copy(y_hbm, pos_ref[2 * t], ybuf.at[to_slot], r, sems.at[to_slot]).start(priority=0)
            _row_copy(y_hbm, pos_ref[2 * t + 1], ybuf.at[to_slot], tr + r, sems.at[to_slot]).start(priority=1)
            return carry
        lax.fori_loop(0, tr, body, 0, unroll=8)

    @pl.when(i == 0)
    def _():
        issue(0, 0)

    @pl.when(i + 1 < n_tiles)
    def _():
        issue(i + 1, 1 - slot)

    pltpu.make_async_copy(y_hbm.at[pl.ds(0, 2 * tr)], ybuf.at[slot], sems.at[slot]).wait()
    y = ybuf[slot, :tr] + ybuf[slot, tr:]
    xn, h = _deepnorm(x_ref[...], y, v_ref[0])
    xo_ref[...] = xn
    if h_refs:
        h_refs[0][...] = h.astype(h_refs[0].dtype)


def combine_ln(x, y_sorted, pair_pos, vecs, n_lat, rows, emit_h=True):
    d = x.shape[1]
    tr = _pick(math.gcd(rows, n_lat), (128, 64, 32, 16, 8))
    n_tiles = rows // tr
    n_lat_tiles = n_lat // tr
    row = pl.BlockSpec((tr, d), lambda i, pos: (i, 0))
    vec = pl.BlockSpec((1, 8, d), lambda i, pos: ((i >= n_lat_tiles).astype(jnp.int32), 0, 0))
    out_shape = [jax.ShapeDtypeStruct((rows, d), F32)]
    out_specs = [row]
    if emit_h:
        out_shape.append(jax.ShapeDtypeStruct((rows, d), BF16))
        out_specs.append(row)
    return pl.pallas_call(
        functools.partial(_combine_ln_kernel, tr=tr, n_tiles=n_tiles),
        grid_spec=pltpu.PrefetchScalarGridSpec(
            num_scalar_prefetch=1, grid=(n_tiles,),
            in_specs=[row, pl.BlockSpec(memory_space=pl.ANY), vec],
            out_specs=out_specs,
            scratch_shapes=[pltpu.VMEM((2, 2 * tr, d), F32), pltpu.SemaphoreType.DMA((2,))]),
        out_shape=out_shape, compiler_params=_params("arbitrary"), name="combine_ln",
    )(pair_pos, x, y_sorted, vecs)


def _mm_kernel(*refs, n_extra, epilogue, transpose_out):
    a_ref, w_ref = refs[0], refs[1]
    extras = refs[2:2 + n_extra]
    o_ref, wb_ref = refs[2 + n_extra], refs[3 + n_extra]

    @pl.when(pl.program_id(1) == 0)
    def _():
        wb_ref[...] = w_ref[...].astype(BF16)

    tm = a_ref.shape[0]
    if epilogue is None or transpose_out:
        acc = jnp.dot(a_ref[...], wb_ref[...], preferred_element_type=F32)
        if transpose_out:
            acc = acc.T
        o_ref[...] = acc.astype(o_ref.dtype)
        return
    half = tm // 2
    for r0 in (0, half):
        rows = pl.ds(r0, half)
        acc = jnp.dot(a_ref[rows, :], wb_ref[...], preferred_element_type=F32)
        acc = epilogue(acc, *[e.at[rows] if e.shape[0] == tm else e for e in extras])
        o_ref[rows, :] = acc.astype(o_ref.dtype)


def matmul(a, w, *, rows, col0, n_cols, out_dtype, epilogue=None, extras=(),
           extra_specs=(), out_mult=1, transpose_out=False, name="matmul"):
    k = a.shape[1]
    tm = _pick(rows, (1024, 768, 512, 256, 128))
    tn = _pick(math.gcd(n_cols, col0) if col0 else n_cols, (512, 256, 128))
    cb0 = col0 // tn
    kern = functools.partial(_mm_kernel, n_extra=len(extras), epilogue=epilogue, transpose_out=transpose_out)
    if transpose_out:
        out_spec = pl.BlockSpec((tn * out_mult, tm), lambda j, i: (j, i))
        out_shape = jax.ShapeDtypeStruct((n_cols * out_mult, rows), out_dtype)
    else:
        out_spec = pl.BlockSpec((tm, tn * out_mult), lambda j, i: (i, j))
        out_shape = jax.ShapeDtypeStruct((rows, n_cols * out_mult), out_dtype)
    return pl.pallas_call(
        kern,
        grid=(n_cols // tn, rows // tm),
        in_specs=[pl.BlockSpec((tm, k), lambda j, i: (i, 0)),
                  pl.BlockSpec((k, tn), lambda j, i: (0, cb0 + j)),
                  *extra_specs],
        out_specs=out_spec,
        out_shape=out_shape,
        scratch_shapes=[pltpu.VMEM((k, tn), BF16)],
        compiler_params=_params("parallel", "arbitrary"),
        name=name,
    )(a, w, *extras)


def _swap_pairs(x, half):
    n = x.shape[-1]
    lane = lax.broadcasted_iota(jnp.int32, x.shape, x.ndim - 1)
    fwd = pltpu.roll(x, n - half, axis=x.ndim - 1)
    bwd = pltpu.roll(x, half, axis=x.ndim - 1)
    return jnp.where((lane % (2 * half)) < half, fwd, bwd)


def _rope(x, cos, sin_signed, half):
    return x * cos + _swap_pairs(x, half) * sin_signed


def _rms(x, gain):
    return x * lax.rsqrt(jnp.mean(x * x, axis=-1, keepdims=True) + EPS) * gain


def _head_epilogue(acc, cos_ref, sin_ref, gain_ref, *, half, scale, norm, rope_chunks, split_halves):
    outs = []
    for c in range(acc.shape[1] // LANES):
        x = acc[:, c * LANES:(c + 1) * LANES]
        if norm:
            ssq = jnp.dot((x * x).astype(BF16), jnp.ones((LANES, LANES), BF16), preferred_element_type=F32)
            x = x * lax.rsqrt(ssq * (1.0 / LANES) + EPS) * gain_ref[...]
        if rope_chunks is None or (c % rope_chunks[1]) == rope_chunks[0]:
            x = _rope(x, cos_ref[...], sin_ref[...], half)
        if scale != 1.0:
            x = x * scale
        if split_halves:
            lane = lax.broadcasted_iota(jnp.int32, x.shape, 1)
            outs.append(jnp.where(lane < LANES // 2, x, 0.0))
            outs.append(jnp.where(lane >= LANES // 2, x, 0.0))
        else:
            outs.append(x)
    return jnp.concatenate(outs, axis=1) if len(outs) > 1 else outs[0]


def project_heads(a, w, tables, gain, *, rows, col0, n_cols, half, scale, norm=False,
                  rope_chunks=None, split_halves=False, name="proj"):
    cos, sin = tables
    tm = _pick(rows, (1024, 768, 512, 256, 128))
    if gain is None:
        gain = jnp.ones((1, LANES), F32)
    epi = functools.partial(_head_epilogue, half=half, scale=scale, norm=norm, rope_chunks=rope_chunks,
                            split_halves=split_halves)
    return matmul(
        a, w, rows=rows, col0=col0, n_cols=n_cols, out_dtype=BF16, epilogue=epi,
        extras=(cos, sin, gain),
        extra_specs=(pl.BlockSpec((tm, LANES), lambda j, i: (i, 0)),
                     pl.BlockSpec((tm, LANES), lambda j, i: (i, 0)),
                     pl.BlockSpec((1, LANES), lambda j, i: (0, 0))),
        out_mult=2 if split_halves else 1, name=name)


def _rms_mm_kernel(*refs, n_extra, epilogue, transpose_out):
    a_ref, g_ref, w_ref = refs[:3]
    extras = refs[3:3 + n_extra]
    o_ref, an_ref = refs[3 + n_extra:]

    @pl.when(pl.program_id(1) == 0)
    def _():
        an_ref[...] = _rms(a_ref[...], g_ref[...]).astype(BF16)

    acc = jnp.dot(an_ref[...], w_ref[...].astype(BF16), preferred_element_type=F32)
    if epilogue is not None:
        acc = epilogue(acc, *extras)
    if transpose_out:
        acc = acc.T
    o_ref[...] = acc.astype(o_ref.dtype)


def rms_matmul(a, gain, w, *, rows, n_cols, out_dtype, epilogue=None, extras=(), extra_specs=(),
               transpose_out=False, name="rms_mm"):
    k = a.shape[1]
    tm = _pick(rows, (1024, 768, 512, 256, 128))
    tn = _pick(n_cols, (512, 256, 128))
    kern = functools.partial(_rms_mm_kernel, n_extra=len(extras), epilogue=epilogue, transpose_out=transpose_out)
    if transpose_out:
        out_spec = pl.BlockSpec((tn, tm), lambda i, j: (j, i))
        out_shape = jax.ShapeDtypeStruct((n_cols, rows), out_dtype)
    else:
        out_spec = pl.BlockSpec((tm, tn), lambda i, j: (i, j))
        out_shape = jax.ShapeDtypeStruct((rows, n_cols), out_dtype)
    return pl.pallas_call(
        kern,
        grid=(rows // tm, n_cols // tn),
        in_specs=[pl.BlockSpec((tm, k), lambda i, j: (i, 0)),
                  pl.BlockSpec((1, k), lambda i, j: (0, 0)),
                  pl.BlockSpec((k, tn), lambda i, j: (0, j)),
                  *extra_specs],
        out_specs=out_spec,
        out_shape=out_shape,
        scratch_shapes=[pltpu.VMEM((tm, k), BF16)],
        compiler_params=_params("parallel", "arbitrary"),
        name=name,
    )(a, gain.reshape(1, k), w, *extras)


def _kq(k, q):
    return lax.dot_general(k, q, (((1,), (1,)), ((), ())), preferred_element_type=F32)


def _col_reduce(x, op):
    rows, n = x.shape
    parts = 8 if rows % 64 == 0 else 1
    if parts > 1:
        x = op(x.reshape(parts, rows // parts, n), axis=1)
    return op(x, axis=0, keepdims=True)


def _flash_kernel(*refs, n_kv, G, dq, has_k2, diff, nq_lat, nk, kc, qcols):
    refs = list(refs)
    if diff is not None:
        lam_ref, sub_ref = refs.pop(0), refs.pop(0)
    q_ref, k_ref = refs.pop(0), refs.pop(0)
    k2_ref = refs.pop(0) if has_k2 else None
    vt_ref, kx_ref = refs.pop(0), refs.pop(0)
    kx2_ref = refs.pop(0) if has_k2 else None
    vxt_ref = refs.pop(0)
    o_ref, m_scr, acc_scr = refs
    i, j = pl.program_id(1), pl.program_id(2)
    tq = q_ref.shape[0]

    def folded_q(s):
        parts = [q_ref[:, (s * G + g) * dq:(s * G + g + 1) * dq] for g in range(G)]
        return jnp.concatenate(parts, axis=0) if G > 1 else parts[0]

    def scores(qs, k_r, k2_r, s, rows, cols):
        k = k_r[rows, s * LANES:(s + 1) * LANES]
        if k2_r is not None:
            k = jnp.concatenate([k, k2_r[rows, :]], axis=1)
        return _kq(k, qs[s][cols])

    def update(s, cols, st, vt):
        m_prev = m_scr[s, :, cols]
        m_new = jnp.maximum(m_prev, _col_reduce(st, jnp.max))
        p = jnp.exp2((st - m_new).astype(BF16))
        alpha = jnp.exp2(m_prev - m_new)
        vt1 = jnp.concatenate([vt, jnp.ones((ONES_ROWS, vt.shape[1]), BF16)], axis=0)
        acc_scr[s, :, cols] = acc_scr[s, :, cols] * alpha + jnp.dot(vt1, p, preferred_element_type=F32)
        m_scr[s, :, cols] = m_new

    def run(k_r, k2_r, vt_r, row_slices):
        qs = [folded_q(s) for s in range(n_kv)]
        col_slices = [slice(c0, c0 + qcols) for c0 in range(0, G * tq, qcols)]
        items = [(s, rows, cols) for rows in row_slices for s in range(n_kv) for cols in col_slices]
        ahead = [scores(qs, k_r, k2_r, *it) for it in items[:SCORE_LOOKAHEAD]]
        for t, (s, rows, cols) in enumerate(items):
            st = ahead.pop(0)
            if t + SCORE_LOOKAHEAD < len(items):
                ahead.append(scores(qs, k_r, k2_r, *items[t + SCORE_LOOKAHEAD]))
            update(s, cols, st, vt_r[s * LANES:(s + 1) * LANES, rows])

    @pl.when(j == 0)
    def _():
        m_scr[...] = jnp.full(m_scr.shape, NEG_INF, F32)
        acc_scr[...] = jnp.zeros(acc_scr.shape, F32)
        run(kx_ref, kx2_ref, vxt_ref, [slice(None)])

    @pl.when(i < nq_lat)
    def _():
        run(k_ref, k2_ref, vt_ref, [slice(cc * kc, (cc + 1) * kc) for cc in range(k_ref.shape[0] // kc)])

    def normalized(s):
        return acc_scr[s, :LANES] / acc_scr[s, LANES:LANES + 1]

    @pl.when(j == nk - 1)
    def _():
        if diff is None:
            for s in range(n_kv):
                o = normalized(s)
                for g in range(G):
                    c = s * G + g
                    o_ref[:, c * LANES:(c + 1) * LANES] = o[:, g * tq:(g + 1) * tq].T.astype(o_ref.dtype)
        else:
            lv = lam_ref[...]
            lam = (jnp.exp(jnp.sum(lv[0:1] * lv[1:2], axis=1, keepdims=True))
                   - jnp.exp(jnp.sum(lv[2:3] * lv[3:4], axis=1, keepdims=True)) + diff)
            for s in range(n_kv):
                o = normalized(s)
                o = o[:, :tq] - lam * o[:, tq:]
                o = _rms(o.T, sub_ref[...]) * (1.0 - diff)
                o_ref[:, s * LANES:(s + 1) * LANES] = o.astype(o_ref.dtype)


def flash_attention(q, k, vt, *, tq, n_lat, n_ctx, with_ctx_q, name, **kw):
    kw.update(n_lat=n_lat, n_ctx=n_ctx)
    if tq == n_ctx or not with_ctx_q:
        return _flash_call(q, k, vt, tq=tq, qb0=0, nq_lat=n_lat // tq, n_ctx_q=1 if with_ctx_q else 0,
                           name=name, **kw)
    o_lat = _flash_call(q, k, vt, tq=tq, qb0=0, nq_lat=n_lat // tq, n_ctx_q=0, name=name, **kw)
    o_ctx = _flash_call(q, k, vt, tq=n_ctx, qb0=n_lat // n_ctx, nq_lat=0, n_ctx_q=1, name=name + "_ctx", **kw)
    return jnp.concatenate([o_lat, o_ctx], axis=0)


def _flash_call(q, k, vt, *, k2=None, diff=None, n_groups, n_kv, G, dq, n_lat, n_ctx, tq, qb0, nq_lat,
                n_ctx_q, name):
    nqb = nq_lat + n_ctx_q
    tk = _pick(n_lat, (ITEMS_PER_STEP * KEY_CHUNK // n_kv, 1024, 512, 256, 128))
    nk = n_lat // tk if nq_lat else 1
    kc = min(tk, KEY_CHUNK)
    cb = n_lat // n_ctx
    kw = n_kv * LANES
    n_slots = n_kv * G
    ow = (n_kv if diff is not None else n_slots) * LANES

    def jmap(i, j):
        return jnp.where(i < nq_lat, j, 0)

    args, specs = [], []
    if diff is not None:
        lam_vecs, subln, lambda_init = diff
        args += [lam_vecs, subln.reshape(1, LANES)]
        specs += [pl.BlockSpec(lam_vecs.shape, lambda h, i, j: (0, 0)),
                  pl.BlockSpec((1, LANES), lambda h, i, j: (0, 0))]
    args += [q, k]
    specs += [pl.BlockSpec((tq, n_slots * dq), lambda h, i, j: (qb0 + i, h)),
              pl.BlockSpec((tk, kw), lambda h, i, j: (jmap(i, j), h))]
    if k2 is not None:
        args.append(k2)
        specs.append(pl.BlockSpec((tk, LANES), lambda h, i, j: (jmap(i, j), 0)))
    args += [vt, k]
    specs += [pl.BlockSpec((kw, tk), lambda h, i, j: (h, jmap(i, j))),
              pl.BlockSpec((n_ctx, kw), lambda h, i, j: (cb, h))]
    if k2 is not None:
        args.append(k2)
        specs.append(pl.BlockSpec((n_ctx, LANES), lambda h, i, j: (cb, 0)))
    args.append(vt)
    specs.append(pl.BlockSpec((kw, n_ctx), lambda h, i, j: (h, cb)))
    kern = functools.partial(_flash_kernel, n_kv=n_kv, G=G, dq=dq, has_k2=k2 is not None,
                             diff=None if diff is None else diff[2],
                             nq_lat=nq_lat, nk=nk, kc=kc, qcols=min(G * tq, QUERY_COLS))
    return pl.pallas_call(
        kern,
        grid=(n_groups, nqb, nk),
        in_specs=specs,
        out_specs=pl.BlockSpec((tq, ow), lambda h, i, j: (i, h)),
        out_shape=jax.ShapeDtypeStruct((nqb * tq, n_groups * ow), BF16),
        scratch_shapes=[pltpu.VMEM((n_kv, 1, G * tq), F32),
                        pltpu.VMEM((n_kv, LANES + ONES_ROWS, G * tq), F32)],
        compiler_params=_params("parallel", "parallel", "arbitrary"),
        name=name,
    )(*args)


def _window_kernel(sink_ref, q_ref, kp_ref, kc_ref, kn_ref, kx_ref, vp_ref, vc_ref, vn_ref, vx_ref,
                   o_ref, *, G, tq, nq):
    h, i = pl.program_id(0), pl.program_id(1)
    q = jnp.concatenate([q_ref[:, g * HEAD_DIM:(g + 1) * HEAD_DIM] for g in range(G)], axis=0)
    k = jnp.concatenate([kp_ref[...], kc_ref[...], kn_ref[...], kx_ref[...]], axis=0)
    vt = jnp.concatenate([vp_ref[...], vc_ref[...], vn_ref[...], vx_ref[...]], axis=1)
    n_keys, n_q = k.shape[0], G * tq
    n_band = tq + 2 * WINDOW
    r = lax.broadcasted_iota(jnp.int32, (n_keys, tq), 0)
    qi = lax.broadcasted_iota(jnp.int32, (n_keys, tq), 1)
    rel = r - WINDOW - qi
    kpos = i * tq - WINDOW + r
    band = (jnp.abs(rel) <= WINDOW) & (kpos >= 0) & (kpos < nq * tq) & (i * tq + qi < nq * tq)
    bias = jnp.where(band | (r >= n_band), 0.0, NEG_INF)
    st = _kq(k, q) + jnp.concatenate([bias] * G, axis=1)
    head = lax.broadcasted_iota(jnp.int32, (1, n_q), 1) // tq
    sink = jnp.zeros((1, n_q), F32)
    for g in range(G):
        sink = jnp.where(head == g, sink_ref[h * G + g] * LOG2E, sink)
    m = jnp.maximum(_col_reduce(st, jnp.max), sink)
    p = jnp.exp2((st - m).astype(BF16))
    vt1 = jnp.concatenate([vt, jnp.ones((ONES_ROWS, n_keys), BF16)], axis=0)
    acc = jnp.dot(vt1, p, preferred_element_type=F32)
    o = acc[:LANES] / (acc[LANES:LANES + 1] + jnp.exp2(sink - m))
    for g in range(G):
        o_ref[:, g * HEAD_DIM:(g + 1) * HEAD_DIM] = o[:, g * tq:(g + 1) * tq].T.astype(o_ref.dtype)


def window_attention(q, k, vt, sink, *, n_heads_kv, G, n_lat, n_ctx, with_ctx_q):
    tq = 2 * WINDOW
    assert n_ctx == tq
    nq = n_lat // tq
    nqb = nq + (1 if with_ctx_q else 0)
    nb = n_lat // WINDOW
    cb = n_lat // n_ctx
    w = WINDOW

    def prev_map(h, i):
        return (jnp.maximum(2 * i - 1, 0), h)

    def next_map(h, i):
        return (jnp.minimum(2 * i + 2, nb - 1), h)

    k_specs = [pl.BlockSpec((w, HEAD_DIM), prev_map),
               pl.BlockSpec((tq, HEAD_DIM), lambda h, i: (i, h)),
               pl.BlockSpec((w, HEAD_DIM), next_map),
               pl.BlockSpec((n_ctx, HEAD_DIM), lambda h, i: (cb, h))]
    vt_specs = [pl.BlockSpec((HEAD_DIM, w), lambda h, i: prev_map(h, i)[::-1]),
                pl.BlockSpec((HEAD_DIM, tq), lambda h, i: (h, i)),
                pl.BlockSpec((HEAD_DIM, w), lambda h, i: next_map(h, i)[::-1]),
                pl.BlockSpec((HEAD_DIM, n_ctx), lambda h, i: (h, cb))]
    kern = functools.partial(_window_kernel, G=G, tq=tq, nq=nq)
    return pl.pallas_call(
        kern,
        grid=(n_heads_kv, nqb),
        in_specs=[pl.BlockSpec(memory_space=pltpu.SMEM),
                  pl.BlockSpec((tq, G * HEAD_DIM), lambda h, i: (i, h)),
                  *k_specs, *vt_specs],
        out_specs=pl.BlockSpec((tq, G * HEAD_DIM), lambda h, i: (i, h)),
        out_shape=jax.ShapeDtypeStruct((nqb * tq, n_heads_kv * G * HEAD_DIM), BF16),
        compiler_params=_params("parallel", "parallel"),
        name="window_attention",
    )(sink, q, k, k, k, k, vt, vt, vt, vt)


def moe_plan(hx, d, tm):
    rows = hx.shape[0]
    n_pairs = 2 * rows
    n_tiles = n_pairs // tm + N_EXPERTS
    e_flat = hx[:, d + N_EXPERTS:d + N_EXPERTS + 2].astype(jnp.int32).reshape(n_pairs)
    order = jnp.argsort(e_flat, stable=True).astype(jnp.int32)
    rank = jnp.argsort(order).astype(jnp.int32)
    counts = jnp.sum(e_flat[:, None] == jnp.arange(N_EXPERTS, dtype=jnp.int32)[None, :], axis=0, dtype=jnp.int32)
    tiles_e = (counts + tm - 1) // tm
    tile_end = jnp.cumsum(tiles_e)
    tile_start = tile_end - tiles_e
    seg_start = jnp.cumsum(counts) - counts
    pair_slot = tile_start[e_flat] * tm + rank - seg_start[e_flat]
    n_used = tile_end[-1]
    tile_ids = jnp.minimum(jnp.arange(n_tiles, dtype=jnp.int32), n_used - 1)
    tile_expert = jnp.sum(tile_ids[:, None] >= tile_end[None, :], axis=1, dtype=jnp.int32)
    e_slot = jnp.repeat(tile_expert, tm)
    r_slot = jnp.arange(n_tiles * tm, dtype=jnp.int32) - tile_start[e_slot] * tm
    valid = (r_slot < counts[e_slot]) & (jnp.arange(n_tiles * tm, dtype=jnp.int32) < n_used * tm)
    src_pair = order[jnp.clip(seg_start[e_slot] + r_slot, 0, n_pairs - 1)]
    src_tok = jnp.where(valid, src_pair // 2, 0)
    return src_tok, tile_expert, n_used.reshape(1), pair_slot


def _moe_experts_kernel(te_ref, src_ref, n_ref, hx_hbm, wgu_ref, wd_ref, o_ref, xbuf, sems, wgu_bf, wd_bf,
                        *, tm, d):
    i = pl.program_id(0)
    n_used = n_ref[0]
    slot = i % 2

    def issue(tile, to_slot):
        def body(r2, carry):
            for prio in (0, 1):
                r = 2 * r2 + prio
                _row_copy(hx_hbm, src_ref[tile * tm + r], xbuf.at[to_slot], r, sems.at[to_slot]).start(priority=prio)
            return carry
        lax.fori_loop(0, tm // 2, body, 0, unroll=4)

    @pl.when(i == 0)
    def _():
        issue(0, 0)

    @pl.when(i + 1 < n_used)
    def _():
        issue(i + 1, 1 - slot)

    @pl.when(i < n_used)
    def _():
        e = te_ref[i]

        @pl.when((i == 0) | (e != te_ref[jnp.maximum(i - 1, 0)]))
        def _():
            wgu_bf[...] = wgu_ref[0, 0].astype(BF16)
            wd_bf[...] = wd_ref[0, 0].astype(BF16)

        pltpu.make_async_copy(hx_hbm.at[pl.ds(0, tm)], xbuf.at[slot], sems.at[slot]).wait()
        xg = xbuf[slot]
        gu = jnp.dot(xg[:, :d].astype(BF16), wgu_bf[...], preferred_element_type=F32)
        f = gu.shape[1] // 2
        gate, up = gu[:, :f], gu[:, f:]
        comb = xg[:, d:]
        lane = lax.broadcasted_iota(jnp.int32, comb.shape, 1)
        c = jnp.sum(jnp.where(lane == e, comb, 0.0), axis=1, keepdims=True)
        act = (gate * jax.nn.sigmoid(gate)) * up * c
        o_ref[...] = jnp.dot(act.astype(BF16), wd_bf[...], preferred_element_type=F32)

    @pl.when(i >= n_used)
    def _():
        o_ref[...] = jnp.zeros(o_ref.shape, o_ref.dtype)


def moe_experts(hx, plan, w_gate_up, w_down, layer, tm):
    src_tok, tile_expert, n_used, _ = plan
    d = hx.shape[1] - LANES
    f2 = w_gate_up.shape[3]
    n_tiles = tile_expert.shape[0]
    return pl.pallas_call(
        functools.partial(_moe_experts_kernel, tm=tm, d=d),
        grid_spec=pltpu.PrefetchScalarGridSpec(
            num_scalar_prefetch=3, grid=(n_tiles,),
            in_specs=[pl.BlockSpec(memory_space=pl.ANY),
                      pl.BlockSpec((1, 1, d, f2), lambda i, te, src, n: (layer, te[i], 0, 0)),
                      pl.BlockSpec((1, 1, f2 // 2, d), lambda i, te, src, n: (layer, te[i], 0, 0))],
            out_specs=pl.BlockSpec((tm, d), lambda i, te, src, n: (i, 0)),
            scratch_shapes=[pltpu.VMEM((2, tm, d + LANES), F32), pltpu.SemaphoreType.DMA((2,)),
                            pltpu.VMEM((d, f2), BF16), pltpu.VMEM((f2 // 2, d), BF16)]),
        out_shape=jax.ShapeDtypeStruct((n_tiles * tm, d), F32),
        compiler_params=_params("arbitrary"), name="moe_experts",
    )(tile_expert, src_tok, n_used, hx, w_gate_up, w_down)


def _rope_tables(n_lat, n_ctx, rot_dim):
    rows = n_lat // GRID_W
    r, col = jnp.meshgrid(jnp.arange(rows, dtype=F32), jnp.arange(GRID_W, dtype=F32), indexing="ij")
    pos = jnp.stack([r.reshape(-1), col.reshape(-1)], axis=-1)
    n_freq = rot_dim // 4
    inv_freq = ROPE_BASE ** (-jnp.arange(n_freq, dtype=F32) / n_freq)
    ang = pos[:, :, None] * inv_freq
    cos, sin = jnp.cos(ang), jnp.sin(ang)
    cos_full = jnp.concatenate([cos[:, 0], cos[:, 0], cos[:, 1], cos[:, 1]], axis=-1)
    sin_full = jnp.concatenate([-sin[:, 0], sin[:, 0], -sin[:, 1], sin[:, 1]], axis=-1)
    reps = LANES // rot_dim
    cos_full = jnp.tile(cos_full, (1, reps))
    sin_full = jnp.tile(sin_full, (1, reps))
    cos_full = jnp.concatenate([cos_full, jnp.ones((n_ctx, LANES), F32)], axis=0)
    sin_full = jnp.concatenate([sin_full, jnp.zeros((n_ctx, LANES), F32)], axis=0)
    return cos_full, sin_full


def _vec_pack(mods, gate_idx, ln_g, ln_b, next_mods, shift_idx, scale_idx):
    d = mods.shape[-1]
    z = jnp.zeros((2, d), F32)
    gate = mods[:2, gate_idx] if gate_idx is not None else z
    g = jnp.broadcast_to(ln_g, (2, d)) if ln_g is not None else z
    b = jnp.broadcast_to(ln_b, (2, d)) if ln_b is not None else z
    shift = next_mods[:2, shift_idx] if next_mods is not None else z
    scale = next_mods[:2, scale_idx] if next_mods is not None else z
    return jnp.stack([gate, g, b, shift, scale, z, z, z], axis=1)


def kernel(x, c, ctx, c_ctx, ada_w, ada_b, ln_g, ln_b, win_w_qkv, win_w_o, win_sink, qkn_w_qkv, qkn_q_gain, qkn_k_gain, qkn_w_o, mla_w_a, mla_q_gain, mla_kv_gain, mla_w_qb, mla_w_kvb, mla_w_o, diff_w_qkv, diff_lambda, diff_subln, diff_w_o, moe_w_group, moe_b_group, moe_w_expert, moe_b_expert, moe_w_gate_up, moe_w_down):
    b, n_lat, d = x.shape
    n_ctx = ctx.shape[1]
    assert b == 1
    t = n_lat + n_ctx
    n_heads = d // HEAD_DIM
    n_kv = n_heads // 4
    grp = n_heads // n_kv

    tab_h = _rope_tables(n_lat, n_ctx, HEAD_DIM)
    tab_r = _rope_tables(n_lat, n_ctx, C_ROPE)

    cond8 = jnp.zeros((8, d), F32).at[0].set(c[0]).at[1].set(c_ctx)
    mods = modulation_all(cond8, ada_w, ada_b).reshape(DEPTH, 8, N_MOD, d)

    xs = jnp.concatenate([x[0], ctx[0]], axis=0)
    h = modulate(xs, _vec_pack(mods[0], None, None, None, mods[0], 0, 1), n_lat)

    for i in range(DEPTH):
        kind = i % 4
        need_ctx = i < DEPTH - 1
        rows = t if need_ctx else n_lat
        att_kw = dict(n_lat=n_lat, n_ctx=n_ctx, with_ctx_q=need_ctx)
        if kind == 0:
            w = win_w_qkv[0]
            scale = HEAD_DIM ** -0.5 * LOG2E
            q = project_heads(h, w, tab_h, None, rows=rows, col0=0, n_cols=d, half=32, scale=scale, name="win_q")
            k = project_heads(h, w, tab_h, None, rows=t, col0=d, n_cols=n_kv * HEAD_DIM, half=32, scale=1.0,
                              name="win_k")
            vt = matmul(h, w, rows=t, col0=d + n_kv * HEAD_DIM, n_cols=n_kv * HEAD_DIM, out_dtype=BF16,
                        transpose_out=True, name="win_vt")
            o = window_attention(q, k, vt, win_sink[0], n_heads_kv=n_kv, G=grp, **att_kw)
            w_o = win_w_o[0]
        elif kind == 1:
            w = qkn_w_qkv[0]
            scale = HEAD_DIM ** -0.5 * LOG2E
            q = project_heads(h, w, tab_h, qkn_q_gain[0].reshape(1, HEAD_DIM), rows=rows, col0=0, n_cols=d, half=32,
                              scale=scale, norm=True, name="qkn_q")
            k = project_heads(h, w, tab_h, qkn_k_gain[0].reshape(1, HEAD_DIM), rows=t, col0=d,
                              n_cols=n_kv * HEAD_DIM, half=32, scale=1.0, norm=True, name="qkn_k")
            vt = matmul(h, w, rows=t, col0=d + n_kv * HEAD_DIM, n_cols=n_kv * HEAD_DIM, out_dtype=BF16,
                        transpose_out=True, name="qkn_vt")
            o = flash_attention(q, k, vt, tq=n_ctx, n_groups=n_kv, n_kv=1, G=grp, dq=HEAD_DIM, name="qkn_attn",
                                **att_kw)
            w_o = qkn_w_o[0]
        elif kind == 2:
            w_a = mla_w_a[0]
            q_rank = mla_q_gain.shape[1]
            kv_rank = mla_kv_gain.shape[1]
            scale = (C_NOPE + C_ROPE) ** -0.5 * LOG2E
            cq = matmul(h, w_a, rows=rows, col0=0, n_cols=q_rank, out_dtype=F32, name="mla_cq")
            ckv = matmul(h, w_a, rows=t, col0=q_rank, n_cols=kv_rank, out_dtype=F32, name="mla_ckv")
            w_pe = jnp.pad(w_a[:, q_rank + kv_rank:], ((0, 0), (0, LANES - C_ROPE)))
            k_pe = project_heads(h, w_pe, tab_r, None, rows=t, col0=0, n_cols=LANES, half=16, scale=1.0,
                                 name="mla_kpe")
            w_qb = mla_w_qb[0].reshape(q_rank, n_heads, C_NOPE + C_ROPE)
            w_qb = jnp.pad(w_qb, ((0, 0), (0, 0), (0, 2 * LANES - C_NOPE - C_ROPE))).reshape(q_rank, n_heads * 2 * LANES)
            w_kvb = mla_w_kvb[0].reshape(kv_rank, n_heads, C_NOPE + C_V)
            w_kn = w_kvb[:, :, :C_NOPE].reshape(kv_rank, n_heads * C_NOPE)
            w_v = w_kvb[:, :, C_NOPE:].reshape(kv_rank, n_heads * C_V)
            tm = _pick(rows, (1024, 768, 512, 256, 128))
            q_epi = functools.partial(_head_epilogue, half=16, scale=scale, norm=False, rope_chunks=(1, 2),
                                      split_halves=False)
            q = rms_matmul(cq, mla_q_gain[0], w_qb, rows=rows, n_cols=n_heads * 2 * LANES, out_dtype=BF16,
                           epilogue=lambda acc, cs, sn: q_epi(acc, cs, sn, None),
                           extras=tab_r,
                           extra_specs=(pl.BlockSpec((tm, LANES), lambda i, j: (i, 0)),
                                        pl.BlockSpec((tm, LANES), lambda i, j: (i, 0))), name="mla_q")
            k_nope = rms_matmul(ckv, mla_kv_gain[0], w_kn, rows=t, n_cols=n_heads * C_NOPE, out_dtype=BF16,
                                name="mla_kn")
            vt = rms_matmul(ckv, mla_kv_gain[0], w_v, rows=t, n_cols=n_heads * C_V, out_dtype=BF16,
                            transpose_out=True, name="mla_vt")
            hp = 2
            o = flash_attention(q, k_nope, vt, tq=_pick(n_lat, (QUERY_COLS, n_ctx)), k2=k_pe,
                                n_groups=n_heads // hp, n_kv=hp, G=1, dq=2 * LANES, name="mla_attn", **att_kw)
            w_o = mla_w_o[0]
        else:
            w = diff_w_qkv[0]
            lambda_init = 0.8 - 0.6 * math.exp(-0.3 * i)
            scale = DF_HEAD ** -0.5 * LOG2E
            n_dh = d // (2 * DF_HEAD)
            q = project_heads(h, w, tab_r, None, rows=rows, col0=0, n_cols=d, half=16, scale=scale,
                              split_halves=True, name="diff_q")
            k = project_heads(h, w, tab_r, None, rows=t, col0=d, n_cols=d, half=16, scale=1.0, name="diff_k")
            vt = matmul(h, w, rows=t, col0=2 * d, n_cols=d, out_dtype=BF16, transpose_out=True, name="diff_vt")
            hp = 2
            o = flash_attention(q, k, vt, tq=_pick(n_lat, (QUERY_COLS // 2, n_ctx)),
                                diff=(diff_lambda[0], diff_subln[0], lambda_init), n_groups=n_dh // hp,
                                n_kv=hp, G=2, dq=LANES, name="diff_attn", **att_kw)
            w_o = diff_w_o[0]

        a = matmul(o, w_o, rows=rows, col0=0, n_cols=d, out_dtype=BF16, name="attn_out")
        w_r = jnp.concatenate([moe_w_expert[i], moe_w_group[i],
                               jnp.zeros((d, LANES - N_EXPERTS - N_GROUPS), F32)], axis=1)
        b_r = jnp.concatenate([moe_b_expert[i], moe_b_group[i],
                               jnp.zeros((LANES - N_EXPERTS - N_GROUPS,), F32)]).reshape(1, LANES)
        xs, hx = ln_router(xs, a, _vec_pack(mods[i], 2, ln_g[i, 0], ln_b[i, 0], mods[i], 3, 4), w_r, b_r,
                           n_lat, rows)
        plan = moe_plan(hx, d, MOE_TILE)
        y_slots = moe_experts(hx, plan, moe_w_gate_up, moe_w_down, i, MOE_TILE)
        if i + 1 < DEPTH:
            xs, h = combine_ln(xs, y_slots, plan[3], _vec_pack(mods[i], 5, ln_g[i, 1], ln_b[i, 1], mods[i + 1], 0, 1),
                               n_lat, rows)
        else:
            (xs,) = combine_ln(xs, y_slots, plan[3], _vec_pack(mods[i], 5, ln_g[i, 1], ln_b[i, 1], None, 0, 1),
                               n_lat, rows, emit_h=False)
    return xs[:n_lat].reshape(b, n_lat, d)
```
